```python
import jax, jax.numpy as jnp
from jax import lax
import numpy as np

D_MODEL = 1024
BATCH = 8
SEQ = 2048
DEPTH = 4
DEC_BATCH = 32
DEC_SEQ = 1
PAST_LEN = 8192
PAGE_SIZE = 128

ML_DH = 64
ML_HEADS = D_MODEL // 4 // ML_DH
ML_W = ML_HEADS * ML_DH
GLA_DK = 32
GLA_DV = 64
GLA_HEADS = D_MODEL // 4 // GLA_DV
GLA_KW = GLA_HEADS * GLA_DK
GLA_VW = GLA_HEADS * GLA_DV
GLA_RANK = 16
GLA_TAU = 16.0
FOX_DH = 64
FOX_HEADS = D_MODEL // 2 // FOX_DH
FOX_W = FOX_HEADS * FOX_DH
MIX_W = ML_W + GLA_VW + FOX_W
CHUNK = 128
Q_BLOCK = 128
N_GROUPS = 4
EXPERTS_PER_GROUP = 4
N_EXPERTS = N_GROUPS * EXPERTS_PER_GROUP
TOP_K = 2
D_EXPERT = D_MODEL // 4
DN_ALPHA = (2.0 * DEPTH) ** 0.25
DN_BETA = (8.0 * DEPTH) ** -0.25
LN_EPS = 1e-5
NORM_EPS = 1e-6
POOL_NUM = 5
POOL_DEN = 4

IN_SIZES = (ML_W, ML_W, ML_W, ML_W, ML_HEADS, ML_HEADS,
            GLA_KW, GLA_KW, GLA_VW, GLA_VW, GLA_RANK,
            FOX_W, FOX_W, FOX_W, FOX_HEADS)
N_IN = sum(IN_SIZES)
SPLIT_POINTS = tuple(sum(IN_SIZES[:i + 1]) for i in range(len(IN_SIZES) - 1))
ML_F_COL = sum(IN_SIZES[:5])
FOX_F_COL = sum(IN_SIZES[:14])

kernel_name = 'hymba_mlstm_gla_fox_hmoe_deepnorm_step'


def layer_norm(x, g, b):
    xf = x.astype(jnp.float32)
    xc = xf - jnp.mean(xf, -1, keepdims=True)
    var = jnp.mean(xc * xc, -1, keepdims=True)
    return (xc * lax.rsqrt(var + LN_EPS) * g + b).astype(x.dtype)


def head_rms_norm(h, g):
    h = h * lax.rsqrt(jnp.mean(h * h, -1, keepdims=True) + NORM_EPS)
    return h.reshape(h.shape[:2] + (-1,)) * g


def scan_chunks(step, state, xs):
    B, S = xs[0].shape[:2]
    nc = S // CHUNK
    xs_c = [jnp.moveaxis(a.reshape((B, nc, CHUNK) + a.shape[2:]), 1, 0) for a in xs]
    state, ys = lax.scan(lambda c, inp: step(c, *inp), state, xs_c)
    return state, jnp.moveaxis(ys, 0, 1).reshape((B, S) + ys.shape[3:])


def mlstm_chunk(state, q, k, v, logi, logf):
    C, n, m = state
    L = q.shape[1]
    b = jnp.cumsum(logf, axis=1).transpose(0, 2, 1)
    li = logi.transpose(0, 2, 1)
    causal = jnp.tril(jnp.ones((L, L), dtype=bool))
    dmat = jnp.where(causal, b[..., :, None] - b[..., None, :] + li[..., None, :], -jnp.inf)
    inter = b + m[..., None]
    m_t = jnp.maximum(inter, jnp.max(dmat, -1))
    w_prev = jnp.exp(inter - m_t)
    s = jnp.einsum('blhd,bshd->bhls', q, k) * jnp.exp(dmat - m_t[..., None])
    num = (jnp.einsum('bhls,bshe->blhe', s, v)
           + jnp.einsum('blhd,bhde->blhe', q, C) * w_prev.transpose(0, 2, 1)[..., None])
    den = jnp.sum(s, -1) + w_prev * jnp.einsum('blhd,bhd->bhl', q, n)
    den = jnp.maximum(jnp.abs(den), jnp.exp(-m_t))
    h = num / den.transpose(0, 2, 1)[..., None]
    m_new = m_t[..., -1]
    g_prev = jnp.exp(b[..., -1] + m - m_new)
    g_rows = jnp.exp(b[..., -1:] - b + li - m_new[..., None])
    C_new = g_prev[..., None, None] * C + jnp.einsum('bhs,bshd,bshe->bhde', g_rows, k, v)
    n_new = g_prev[..., None] * n + jnp.einsum('bhs,bshd->bhd', g_rows, k)
    return (C_new, n_new, m_new), h


def run_mlstm(q, k, v, i_pre, f_pre, o_pre, state, norm_g, chunked):
    B, S = q.shape[:2]
    shp = (B, S, ML_HEADS, ML_DH)
    qf = q.astype(jnp.float32).reshape(shp)
    kf = k.astype(jnp.float32).reshape(shp) * (ML_DH ** -0.5)
    vf = v.astype(jnp.float32).reshape(shp)
    logi = i_pre.astype(jnp.float32)
    logf = jax.nn.log_sigmoid(f_pre.astype(jnp.float32))
    xs = [qf, kf, vf, logi, logf]
    if chunked:
        state, h = scan_chunks(mlstm_chunk, state, xs)
    else:
        state, h = mlstm_chunk(state, *xs)
    y = head_rms_norm(h, norm_g) * jax.nn.sigmoid(o_pre.astype(jnp.float32))
    return y.astype(q.dtype), state


def gla_chunk(S, q, k, v, loga):
    L = q.shape[1]
    bc = jnp.cumsum(loga, axis=1)
    causal = jnp.tril(jnp.ones((L, L), dtype=bool))[None, :, :, None, None]
    decay = jnp.exp(jnp.where(causal, bc[:, :, None] - bc[:, None, :], -jnp.inf))
    att = jnp.einsum('bthd,bshd,btshd->bhts', q, k, decay)
    o = (jnp.einsum('bhts,bshe->bthe', att, v)
         + jnp.einsum('bthd,bhde->bthe', q * jnp.exp(bc), S))
    last = bc[:, -1]
    S_new = (jnp.exp(last)[..., None] * S
             + jnp.einsum('bshd,bshe->bhde', k * jnp.exp(last[:, None] - bc), v))
    return S_new, o


def run_gla(q, k, v, r, g_lr, state, w_gate_up, b_gate, norm_g, chunked):
    B, S = q.shape[:2]
    qf = q.astype(jnp.float32).reshape(B, S, GLA_HEADS, GLA_DK) * (GLA_DK ** -0.5)
    kf = k.astype(jnp.float32).reshape(B, S, GLA_HEADS, GLA_DK)
    vf = v.astype(jnp.float32).reshape(B, S, GLA_HEADS, GLA_DV)
    z = jnp.einsum('bsr,rk->bsk', g_lr, w_gate_up) + b_gate
    loga = (jax.nn.log_sigmoid(z.astype(jnp.float32)) / GLA_TAU).reshape(B, S, GLA_HEADS, GLA_DK)
    xs = [qf, kf, vf, loga]
    if chunked:
        state, o = scan_chunks(gla_chunk, state, xs)
    else:
        state, o = gla_chunk(state, *xs)
    y = head_rms_norm(o, norm_g) * jax.nn.silu(r.astype(jnp.float32))
    return y.astype(q.dtype), state


def fox_attend(q, cq, qpos, k, v, ck, kpos):
    s = jnp.einsum('bqhd,bkhd->bhqk', q, k, preferred_element_type=jnp.float32) * (FOX_DH ** -0.5)
    s = s + jnp.swapaxes(cq, 1, 2)[..., :, None] - jnp.swapaxes(ck, 1, 2)[..., None, :]
    s = jnp.where(kpos[None, :] <= qpos[:, None], s, -jnp.inf)
    p = jax.nn.softmax(s, axis=-1)
    return jnp.einsum('bhqk,bkhd->bqhd', p.astype(v.dtype), v)


def fox_prompt(q, k, v, f_pre):
    B, S = q.shape[:2]
    shp = (B, S, FOX_HEADS, FOX_DH)
    q, k, v = q.reshape(shp), k.reshape(shp), v.reshape(shp)
    logf = jax.nn.log_sigmoid(f_pre.astype(jnp.float32))
    c = jnp.cumsum(logf, axis=1)
    pos = jnp.arange(S)
    nb = S // Q_BLOCK
    qb = jnp.moveaxis(q.reshape(B, nb, Q_BLOCK, FOX_HEADS, FOX_DH), 1, 0)
    cb = jnp.moveaxis(c.reshape(B, nb, Q_BLOCK, FOX_HEADS), 1, 0)
    pb = pos.reshape(nb, Q_BLOCK)
    o = lax.map(lambda blk: fox_attend(blk[0], blk[1], blk[2], k, v, c, pos), (qb, cb, pb))
    o = jnp.moveaxis(o, 0, 1).reshape(B, S, FOX_W)
    return o, k, v, logf


def gather_pages(pool, layer, page_table):
    g = pool[layer, page_table]
    return g.reshape((g.shape[0], g.shape[1] * g.shape[2]) + g.shape[3:])


def fox_sample(q, k, v, f_pre, pool_k, pool_v, pool_logf, layer, page_table):
    B, S = q.shape[:2]
    shp = (B, S, FOX_HEADS, FOX_DH)
    q, k, v = q.reshape(shp), k.reshape(shp), v.reshape(shp)
    logf = jax.nn.log_sigmoid(f_pre.astype(jnp.float32))
    past_k = gather_pages(pool_k, layer, page_table).astype(k.dtype)
    past_v = gather_pages(pool_v, layer, page_table).astype(v.dtype)
    past_logf = gather_pages(pool_logf, layer, page_table).astype(jnp.float32)
    P = past_k.shape[1]
    k_all = jnp.concatenate([past_k, k], 1)
    v_all = jnp.concatenate([past_v, v], 1)
    c = jnp.cumsum(jnp.concatenate([past_logf, logf], 1), axis=1)
    o = fox_attend(q, c[:, P:], P + jnp.arange(S), k_all, v_all, c, jnp.arange(P + S))
    return o.reshape(B, S, FOX_W), k, v, logf


def hier_moe(x, w_grp, b_grp, w_exp, b_exp, w_gate, w_up, w_down):
    T = x.shape[0]
    grp_logits = jnp.dot(x, w_grp, preferred_element_type=jnp.float32) + b_grp
    grp_prob = jax.nn.softmax(grp_logits, -1)
    g_sel = jnp.argmax(grp_logits, -1)
    p_grp = jnp.take_along_axis(grp_prob, g_sel[:, None], -1)
    exp_logits = (jnp.dot(x, w_exp, preferred_element_type=jnp.float32) + b_exp).reshape(
        T, N_GROUPS, EXPERTS_PER_GROUP)
    in_grp = jnp.take_along_axis(exp_logits, g_sel[:, None, None], 1)[:, 0]
    top_val, top_idx = lax.top_k(in_grp, TOP_K)
    gate = p_grp * jax.nn.softmax(top_val, -1)
    flat = g_sel[:, None] * EXPERTS_PER_GROUP + top_idx
    comb = jnp.sum(jax.nn.one_hot(flat, N_EXPERTS, dtype=jnp.float32) * gate[..., None], 1)
    a = jnp.einsum('td,edf->tef', x, w_gate)
    u = jnp.einsum('td,edf->tef', x, w_up)
    hdn = jax.nn.silu(a) * u * comb[..., None].astype(x.dtype)
    return jnp.einsum('tef,efd->td', hdn, w_down)


def setup_inputs(seed: int = 0) -> dict:
    key = jax.random.key(seed)
    ks = jax.random.split(key, 32)
    f32 = jnp.float32
    n_pages = PAST_LEN // PAGE_SIZE
    n_pool = (DEC_BATCH * n_pages * POOL_NUM) // POOL_DEN

    def nrm(k, shape, s):
        return s * jax.random.normal(k, shape, f32)

    fox_f_off = jnp.linspace(1.0, 5.0, FOX_HEADS)
    ml_f_off = jnp.linspace(3.0, 6.0, ML_HEADS)
    b_in = nrm(ks[13], (DEPTH, N_IN), 0.01)
    b_in = b_in.at[:, ML_F_COL:ML_F_COL + ML_HEADS].add(ml_f_off)
    b_in = b_in.at[:, FOX_F_COL:FOX_F_COL + FOX_HEADS].add(fox_f_off)
    page_table = jax.random.permutation(ks[9], n_pool)[:DEC_BATCH * n_pages].reshape(
        DEC_BATCH, n_pages).astype(jnp.int32)
    return {
        'x_prompt': nrm(ks[0], (BATCH, SEQ, D_MODEL), 1.0),
        'x_sample': nrm(ks[1], (DEC_BATCH, DEC_SEQ, D_MODEL), 1.0),
        'cache_fox_k': nrm(ks[2], (DEPTH, n_pool, PAGE_SIZE, FOX_HEADS, FOX_DH), 1.0),
        'cache_fox_v': nrm(ks[3], (DEPTH, n_pool, PAGE_SIZE, FOX_HEADS, FOX_DH), 1.0),
        'cache_fox_logf': jax.nn.log_sigmoid(fox_f_off + nrm(ks[4], (DEPTH, n_pool, PAGE_SIZE, FOX_HEADS), 1.0)),
        'state_mlstm_C': nrm(ks[5], (DEPTH, DEC_BATCH, ML_HEADS, ML_DH, ML_DH), 0.5),
        'state_mlstm_n': nrm(ks[6], (DEPTH, DEC_BATCH, ML_HEADS, ML_DH), 0.5),
        'state_mlstm_m': nrm(ks[7], (DEPTH, DEC_BATCH, ML_HEADS), 1.0),
        'state_gla_S': nrm(ks[8], (DEPTH, DEC_BATCH, GLA_HEADS, GLA_DK, GLA_DV), 0.5),
        'page_table': page_table,
        'ln_in_g': 1.0 + nrm(ks[10], (D_MODEL,), 0.02),
        'ln_in_b': nrm(ks[11], (D_MODEL,), 0.02),
        'w_in': nrm(ks[12], (DEPTH, D_MODEL, N_IN), D_MODEL ** -0.5),
        'b_in': b_in,
        'mlstm_norm_g': 1.0 + nrm(ks[14], (DEPTH, ML_W), 0.02),
        'gla_w_gate_up': nrm(ks[15], (DEPTH, GLA_RANK, GLA_KW), GLA_RANK ** -0.5),
        'gla_b_gate': nrm(ks[16], (DEPTH, GLA_KW), 0.01) + 2.0,
        'gla_norm_g': 1.0 + nrm(ks[17], (DEPTH, GLA_VW), 0.02),
        'w_out': nrm(ks[18], (DEPTH, MIX_W, D_MODEL), DN_BETA * MIX_W ** -0.5),
        'ln1_g': 1.0 + nrm(ks[19], (DEPTH, D_MODEL), 0.02),
        'ln1_b': nrm(ks[20], (DEPTH, D_MODEL), 0.02),
        'w_router_group': nrm(ks[21], (DEPTH, D_MODEL, N_GROUPS), D_MODEL ** -0.5),
        'b_router_group': nrm(ks[22], (DEPTH, N_GROUPS), 0.01),
        'w_router_expert': nrm(ks[23], (DEPTH, D_MODEL, N_EXPERTS), D_MODEL ** -0.5),
        'b_router_expert': nrm(ks[24], (DEPTH, N_EXPERTS), 0.01),
        'w_exp_gate': nrm(ks[25], (DEPTH, N_EXPERTS, D_MODEL, D_EXPERT), D_MODEL ** -0.5),
        'w_exp_up': nrm(ks[26], (DEPTH, N_EXPERTS, D_MODEL, D_EXPERT), D_MODEL ** -0.5),
        'w_exp_down': nrm(ks[27], (DEPTH, N_EXPERTS, D_EXPERT, D_MODEL), DN_BETA * D_EXPERT ** -0.5),
        'ln2_g': 1.0 + nrm(ks[28], (DEPTH, D_MODEL), 0.02),
        'ln2_b': nrm(ks[29], (DEPTH, D_MODEL), 0.02),
    }


def reference(x_prompt, x_sample, cache_fox_k, cache_fox_v, cache_fox_logf, state_mlstm_C, state_mlstm_n,
              state_mlstm_m, state_gla_S, page_table, ln_in_g, ln_in_b, w_in, b_in, mlstm_norm_g,
              gla_w_gate_up, gla_b_gate, gla_norm_g, w_out, ln1_g, ln1_b, w_router_group, b_router_group,
              w_router_expert, b_router_expert, w_exp_gate, w_exp_up, w_exp_down, ln2_g, ln2_b):
    f32 = jnp.float32

    def layer(h, l, chunked, ml_state, gla_state, fox_fn):
        z = jnp.einsum('bsd,dn->bsn', h, w_in[l]) + b_in[l]
        (ml_q, ml_k, ml_v, ml_o, ml_i, ml_f, g_q, g_k, g_v, g_r, g_lr,
         fx_q, fx_k, fx_v, fx_f) = jnp.split(z, SPLIT_POINTS, axis=-1)
        y_ml, ml_state = run_mlstm(ml_q, ml_k, ml_v, ml_i, ml_f, ml_o, ml_state, mlstm_norm_g[l], chunked)
        y_gla, gla_state = run_gla(g_q, g_k, g_v, g_r, g_lr, gla_state, gla_w_gate_up[l], gla_b_gate[l],
                                   gla_norm_g[l], chunked)
        y_fox, fk, fv, flf = fox_fn(fx_q, fx_k, fx_v, fx_f)
        mix = jnp.einsum('bsm,md->bsd', jnp.concatenate([y_ml, y_gla, y_fox.astype(h.dtype)], -1), w_out[l])
        h = layer_norm(DN_ALPHA * h + mix, ln1_g[l], ln1_b[l])
        nb, ns = h.shape[:2]
        moe = hier_moe(h.reshape(nb * ns, D_MODEL), w_router_group[l], b_router_group[l], w_router_expert[l],
                       b_router_expert[l], w_exp_gate[l], w_exp_up[l], w_exp_down[l]).reshape(h.shape)
        h = layer_norm(DN_ALPHA * h + moe, ln2_g[l], ln2_b[l])
        return h, ml_state, gla_state, fk, fv, flf

    hp = layer_norm(x_prompt, ln_in_g, ln_in_b)
    hs = layer_norm(x_sample, ln_in_g, ln_in_b)
    bp = x_prompt.shape[0]
    kp, vp, lfp, ks_, vs_, lfs = [], [], [], [], [], []
    cp, np_, mp, cs, ns_, ms = [], [], [], [], [], []
    gp, gs = [], []
    for l in range(DEPTH):
        ml0 = (jnp.zeros((bp, ML_HEADS, ML_DH, ML_DH), f32), jnp.zeros((bp, ML_HEADS, ML_DH), f32),
               jnp.zeros((bp, ML_HEADS), f32))
        gla0 = jnp.zeros((bp, GLA_HEADS, GLA_DK, GLA_DV), f32)
        hp, ml_p, gla_p, fk, fv, flf = layer(hp, l, True, ml0, gla0, fox_prompt)
        kp.append(fk); vp.append(fv); lfp.append(flf)
        cp.append(ml_p[0]); np_.append(ml_p[1]); mp.append(ml_p[2]); gp.append(gla_p)

        ml_in = (state_mlstm_C[l].astype(f32), state_mlstm_n[l].astype(f32), state_mlstm_m[l].astype(f32))
        gla_in = state_gla_S[l].astype(f32)
        fox_fn = lambda q, k, v, f, l=l: fox_sample(q, k, v, f, cache_fox_k, cache_fox_v, cache_fox_logf,
                                                     l, page_table)
        hs, ml_s, gla_s, fk, fv, flf = layer(hs, l, False, ml_in, gla_in, fox_fn)
        ks_.append(fk); vs_.append(fv); lfs.append(flf)
        cs.append(ml_s[0]); ns_.append(ml_s[1]); ms.append(ml_s[2]); gs.append(gla_s)

    kd, vd, ld = cache_fox_k.dtype, cache_fox_v.dtype, cache_fox_logf.dtype
    cd, nd, md, sd = state_mlstm_C.dtype, state_mlstm_n.dtype, state_mlstm_m.dtype, state_gla_S.dtype
    return (hp, hs,
            jnp.stack(kp).astype(kd), jnp.stack(vp).astype(vd), jnp.stack(lfp).astype(ld),
            jnp.stack(ks_).astype(kd), jnp.stack(vs_).astype(vd), jnp.stack(lfs).astype(ld),
            jnp.stack(cp).astype(cd), jnp.stack(np_).astype(nd), jnp.stack(mp).astype(md),
            jnp.stack(cs).astype(cd), jnp.stack(ns_).astype(nd), jnp.stack(ms).astype(md),
            jnp.stack(gp).astype(sd), jnp.stack(gs).astype(sd))
```

```python
import functools

import numpy as np
import jax
import jax.numpy as jnp
from jax import lax
from jax.experimental import pallas as pl
from jax.experimental.pallas import tpu as pltpu

F32 = jnp.float32
BF16 = jnp.bfloat16
HI = lax.Precision.HIGHEST

D_MODEL = 1024
DEPTH = 4
PAGE_SIZE = 128
ML_DH = 64
ML_HEADS = 4
ML_W = 256
GLA_DK = 32
GLA_DV = 64
GLA_HEADS = 4
GLA_KW = 128
GLA_VW = 256
GLA_RANK = 16
GLA_TAU = 16.0
FOX_DH = 64
FOX_HEADS = 8
FOX_W = 512
CHUNK = 128
SUB = 16
N_GROUPS = 4
EXPERTS_PER_GROUP = 4
N_EXPERTS = 16
D_EXPERT = 256
DN_ALPHA = (2.0 * DEPTH) ** 0.25
LN_EPS = 1e-5
NORM_EPS = 1e-6
NEG_INF = float("-inf")

_OFF = dict(ml_q=0, ml_k=256, ml_v=512, ml_o=768, ml_i=1024, ml_f=1028, g_q=1032, g_k=1160, g_v=1288,
            g_r=1544, g_lr=1800, fx_q=1816, fx_k=2328, fx_v=2840, fx_f=3352)
_ML0, _GLA0, _FQ0, _FK0, _FV0, _ZG0, _NPAD = 0, 1024, 1792, 2304, 2816, 3328, 3456
_G_FXF, _G_MLI, _G_MLF, _G_LR = 0, 8, 12, 16
_R_GRP, _R_EXP = 0, 8

VMEM_LIMIT = 56 * 1024 * 1024


def _cp(*sem):
    return pltpu.CompilerParams(dimension_semantics=sem, vmem_limit_bytes=VMEM_LIMIT)


def _dot(a, b, precision=None):
    return jnp.dot(a, b, preferred_element_type=F32, precision=precision)


def _dot_nt(a, b):
    return lax.dot_general(a, b, (((1,), (1,)), ((), ())), preferred_element_type=F32)


def _dot_tn(a, b):
    return lax.dot_general(a, b, (((0,), (0,)), ((), ())), preferred_element_type=F32)


def _log_sigmoid(x):
    return jnp.minimum(x, 0.0) - jnp.log1p(jnp.exp(-jnp.abs(x)))


def _sigmoid(x):
    return 1.0 / (1.0 + jnp.exp(-x))


def _ln_rows(x, g, b):
    mu = jnp.mean(x, axis=-1, keepdims=True)
    xc = x - mu
    var = jnp.mean(xc * xc, axis=-1, keepdims=True)
    return xc * lax.rsqrt(var + LN_EPS) * g + b


def _iota(shape, dim):
    return lax.broadcasted_iota(jnp.int32, shape, dim)


def _ln_kernel(x_ref, g_ref, b_ref, o_ref):
    o_ref[...] = _ln_rows(x_ref[...], g_ref[...], b_ref[...])


def ln_rows(x, g, b, tm):
    T, D = x.shape
    return pl.pallas_call(
        _ln_kernel,
        out_shape=jax.ShapeDtypeStruct((T, D), F32),
        grid=(T // tm,),
        in_specs=[pl.BlockSpec((tm, D), lambda i: (i, 0)),
                  pl.BlockSpec((1, D), lambda i: (0, 0)),
                  pl.BlockSpec((1, D), lambda i: (0, 0))],
        out_specs=pl.BlockSpec((tm, D), lambda i: (i, 0)),
        compiler_params=_cp("parallel"),
        name="ln_in",
    )(x, g, b)


_IN_BOUNDS = ((_ML0, _GLA0), (_GLA0, _FQ0), (_FQ0, _FK0), (_FK0, _FV0), (_FV0, _ZG0), (_ZG0, _NPAD))


def _inproj_kernel(h_ref, w_ref, b_ref, *out_refs):
    xb = h_ref[...].astype(BF16)
    for ref, (a, b) in zip(out_refs, _IN_BOUNDS):
        ref[...] = (_dot(xb, w_ref[:, a:b]) + b_ref[:, a:b]).astype(ref.dtype)


def inproj(h, w, b, tm):
    T = h.shape[0]
    widths = [b_ - a_ for a_, b_ in _IN_BOUNDS]
    dtypes = [F32, F32, BF16, F32, F32, F32]
    return pl.pallas_call(
        _inproj_kernel,
        out_shape=[jax.ShapeDtypeStruct((T, n), dt) for n, dt in zip(widths, dtypes)],
        grid=(T // tm,),
        in_specs=[pl.BlockSpec((tm, D_MODEL), lambda i: (i, 0)),
                  pl.BlockSpec((D_MODEL, _NPAD), lambda i: (0, 0)),
                  pl.BlockSpec((1, _NPAD), lambda i: (0, 0))],
        out_specs=[pl.BlockSpec((tm, n), lambda i: (i, 0)) for n in widths],
        compiler_params=_cp("parallel"),
        name="inproj",
    )(h, w, b)


def _foxgate_kernel(zg_ref, lf_ref, c_ref):
    S = zg_ref.shape[0]
    tri = (_iota((CHUNK, CHUNK), 0) >= _iota((CHUNK, CHUNK), 1)).astype(F32)
    carry = jnp.zeros((1, 128), F32)
    for j in range(S // CHUNK):
        sl = slice(j * CHUNK, (j + 1) * CHUNK)
        lf = _log_sigmoid(zg_ref[sl, :])
        lf_ref[sl, :] = lf[:, _G_FXF:_G_FXF + FOX_HEADS]
        cs = _dot(tri, lf, HI) + carry
        carry = cs[CHUNK - 1:CHUNK, :]
        c_ref[:, sl] = cs.T[_G_FXF:_G_FXF + FOX_HEADS, :]


def foxgate(zg3):
    B, S, _ = zg3.shape
    return pl.pallas_call(
        _foxgate_kernel,
        out_shape=[jax.ShapeDtypeStruct((B, S, FOX_HEADS), F32), jax.ShapeDtypeStruct((B, FOX_HEADS, S), F32)],
        grid=(B,),
        in_specs=[pl.BlockSpec((None, S, 128), lambda b: (b, 0, 0))],
        out_specs=[pl.BlockSpec((None, S, FOX_HEADS), lambda b: (b, 0, 0)),
                   pl.BlockSpec((None, FOX_HEADS, S), lambda b: (b, 0, 0))],
        compiler_params=_cp("parallel"),
        name="foxgate",
    )(zg3)


def _fox_attn_kernel(tq, q_ref, k_ref, v_ref, c_ref, o_ref):
    qi = pl.program_id(2)
    q = q_ref[...]
    lane = _iota((1, 128), 1)
    causal = _iota((tq, tq), 1) <= _iota((tq, tq), 0)
    outs = []
    for hh in range(2):
        qh = jnp.where(lane // FOX_DH == hh, q, jnp.zeros_like(q))

        def step(j, carry, masked, hh=hh, qh=qh):
            m, l, acc = carry
            r0 = pl.multiple_of(j * tq, tq)
            kb = k_ref[pl.ds(r0, tq), :].astype(BF16)
            vb = v_ref[pl.ds(r0, tq), :].astype(BF16)
            ck = c_ref[hh, pl.ds(j, 1), :]
            s = _dot_nt(qh, kb) - ck
            if masked:
                s = jnp.where(causal, s, NEG_INF)
            m_new = jnp.maximum(m, jnp.max(s, axis=1, keepdims=True))
            a = jnp.exp(m - m_new)
            p = jnp.exp(s - m_new)
            l = a * l + jnp.sum(p, axis=1, keepdims=True)
            acc = a * acc + _dot(p.astype(BF16), vb)
            return m_new, l, acc

        init = (jnp.full((tq, 1), NEG_INF, F32), jnp.zeros((tq, 1), F32), jnp.zeros((tq, 128), F32))
        carry = lax.fori_loop(0, qi, functools.partial(step, masked=False), init)
        _, l, acc = step(qi, carry, True)
        outs.append(acc / l)
    o_ref[...] = jnp.where(lane // FOX_DH == 0, outs[0], outs[1]).astype(o_ref.dtype)


def fox_attn(fq, fk, fv, c4, B, S, tq):
    T = B * S
    nq = S // tq
    return pl.pallas_call(
        functools.partial(_fox_attn_kernel, tq),
        out_shape=jax.ShapeDtypeStruct((T, FOX_W), BF16),
        grid=(B, FOX_HEADS // 2, nq),
        in_specs=[pl.BlockSpec((tq, 128), lambda b, hp, qi: (b * nq + qi, hp)),
                  pl.BlockSpec((S, 128), lambda b, hp, qi: (b, hp)),
                  pl.BlockSpec((S, 128), lambda b, hp, qi: (b, hp)),
                  pl.BlockSpec((None, 2, nq, tq), lambda b, hp, qi: (b * (FOX_HEADS // 2) + hp, 0, 0, 0))],
        out_specs=pl.BlockSpec((tq, 128), lambda b, hp, qi: (b * nq + qi, hp)),
        compiler_params=_cp("parallel", "parallel", "arbitrary"),
        name="fox_attn",
    )(fq, fk, fv, c4)


def _mlstm_kernel(n_valid, bb, z_ref, zg_ref, c0_ref, n0_ref, m0_ref, ng_ref,
                  y_ref, co_ref, no_ref, mo_ref, c_s, n_s, m_s):
    c = pl.program_id(1)
    nc = pl.num_programs(1)
    L = CHUNK

    @pl.when(c == 0)
    def _():
        c_s[...] = c0_ref[...]
        n_s[...] = n0_ref[...]
        m_s[...] = m0_ref[...]

    row = _iota((L, L), 0)
    col = _iota((L, L), 1)
    causal = row >= col
    tri = causal.astype(F32)
    lane256 = _iota((1, ML_W), 1)
    lane128 = _iota((1, 128), 1)
    valid = _iota((L, 1), 0) < n_valid
    bd = _iota((ML_W, ML_W), 0) // ML_DH == _iota((ML_W, ML_W), 1) // ML_DH
    seg = (_iota((ML_W, 128), 0) // ML_DH == _iota((ML_W, 128), 1)).astype(F32)
    segmean = bd.astype(F32) * (1.0 / ML_DH)
    ng = ng_ref[...]

    for i in range(bb):
        z = z_ref[i]
        q = z[:, 0:ML_W]
        k = z[:, ML_W:2 * ML_W]
        v = z[:, 2 * ML_W:3 * ML_W]
        og = z[:, 3 * ML_W:4 * ML_W]
        g = zg_ref[i]
        li_all = g
        lf_all = _log_sigmoid(g)
        if n_valid < L:
            li_all = jnp.where(valid, li_all, NEG_INF)
            lf_all = jnp.where(valid, lf_all, 0.0)
        b_all = _dot(tri, lf_all, HI)
        liT = li_all.T
        bT = b_all.T
        qb = q.astype(BF16)
        kb = k.astype(BF16)
        vb = v.astype(BF16)
        C = c_s[i]
        n_row = n_s[i]
        m_row = m_s[i]
        qC = _dot(qb, C.astype(BF16))
        qn = _dot(q * n_row, seg, HI)
        h_acc = jnp.zeros((L, ML_W), F32)
        kg = jnp.zeros((L, ML_W), F32)
        gp_row = jnp.zeros((1, ML_W), F32)
        m_new_row = m_row
        for h in range(ML_HEADS):
            hm = lane256 // ML_DH == h
            b_col = b_all[:, _G_MLF + h:_G_MLF + h + 1]
            b_row = bT[_G_MLF + h:_G_MLF + h + 1, :]
            li_row = liT[_G_MLI + h:_G_MLI + h + 1, :]
            li_col = li_all[:, _G_MLI + h:_G_MLI + h + 1]
            m_prev = m_row[:, h:h + 1]
            dmat = jnp.where(causal, b_col - b_row + li_row, NEG_INF)
            inter = b_col + m_prev
            m_t = jnp.maximum(inter, jnp.max(dmat, axis=1, keepdims=True))
            w_prev = jnp.exp(inter - m_t)
            e = jnp.exp(dmat - m_t)
            qh = jnp.where(hm, q, 0.0).astype(BF16)
            s = _dot_nt(qh, kb) * e
            num = _dot(s.astype(BF16), vb)
            den = jnp.sum(s, axis=1, keepdims=True) + w_prev * qn[:, h:h + 1]
            den = jnp.maximum(jnp.abs(den), jnp.exp(-m_t))
            hfull = (num + qC * w_prev) / den
            h_acc = jnp.where(hm, hfull, h_acc)
            b_last = b_col[L - 1:L, :]
            m_new = m_t[L - 1:L, :]
            g_prev = jnp.exp(b_last + m_prev - m_new)
            g_rows = jnp.exp(b_last - b_col + li_col - m_new)
            kg = jnp.where(hm, k * g_rows, kg)
            gp_row = jnp.where(hm, g_prev, gp_row)
            m_new_row = jnp.where(lane128 == h, m_new, m_new_row)
        upd = _dot_tn(kg.astype(BF16), vb)
        c_s[i] = C * gp_row + jnp.where(bd, upd, 0.0)
        n_s[i] = n_row * gp_row + jnp.sum(kg, axis=0, keepdims=True)
        m_s[i] = m_new_row
        ms = _dot(h_acc * h_acc, segmean, HI)
        y = h_acc * lax.rsqrt(ms + NORM_EPS) * ng * _sigmoid(og)
        y_ref[i] = y.astype(y_ref.dtype)

    @pl.when(c == nc - 1)
    def _():
        co_ref[...] = c_s[...]
        no_ref[...] = n_s[...]
        mo_ref[...] = m_s[...]


def mlstm(z3, zg3, c0, n0, m0, ng, n_valid, bb):
    B, S, _ = z3.shape
    nc = S // CHUNK
    st = lambda shape: pl.BlockSpec((bb,) + shape, lambda b, c: (b, 0, 0))
    return pl.pallas_call(
        functools.partial(_mlstm_kernel, n_valid, bb),
        out_shape=[jax.ShapeDtypeStruct((B, S, ML_W), BF16),
                   jax.ShapeDtypeStruct((B, ML_W, ML_W), F32),
                   jax.ShapeDtypeStruct((B, 1, ML_W), F32),
                   jax.ShapeDtypeStruct((B, 1, 128), F32)],
        grid=(B // bb, nc),
        in_specs=[pl.BlockSpec((bb, CHUNK, 4 * ML_W), lambda b, c: (b, c, 0)),
                  pl.BlockSpec((bb, CHUNK, 128), lambda b, c: (b, c, 0)),
                  st((ML_W, ML_W)), st((1, ML_W)), st((1, 128)),
                  pl.BlockSpec((1, ML_W), lambda b, c: (0, 0))],
        out_specs=[pl.BlockSpec((bb, CHUNK, ML_W), lambda b, c: (b, c, 0)),
                   st((ML_W, ML_W)), st((1, ML_W)), st((1, 128))],
        scratch_shapes=[pltpu.VMEM((bb, ML_W, ML_W), F32), pltpu.VMEM((bb, 1, ML_W), F32),
                        pltpu.VMEM((bb, 1, 128), F32)],
        compiler_params=_cp("parallel", "arbitrary"),
        name="mlstm",
    )(z3, zg3, c0, n0, m0, ng)


def _gla_kernel(n_valid, bb, z_ref, zg_ref, wg_ref, bg_ref, s0_ref, ng_ref, y_ref, so_ref, s_s, o_s):
    c = pl.program_id(1)
    nc = pl.num_programs(1)
    L = CHUNK

    @pl.when(c == 0)
    def _():
        s_s[...] = s0_ref[...]

    tri = (_iota((L, L), 0) >= _iota((L, L), 1)).astype(F32)
    lane128 = _iota((1, GLA_KW), 1)
    lane256 = _iota((1, GLA_VW), 1)
    rowi = _iota((L, 1), 0)
    valid = rowi < n_valid
    bd = _iota((GLA_KW, GLA_VW), 0) // GLA_DK == _iota((GLA_KW, GLA_VW), 1) // GLA_DV
    segexp = bd.astype(BF16)
    bdv = _iota((GLA_VW, GLA_VW), 0) // GLA_DV == _iota((GLA_VW, GLA_VW), 1) // GLA_DV
    segmean = bdv.astype(F32) * (1.0 / GLA_DV)
    ng = ng_ref[...]

    for i in range(bb):
        z = z_ref[i]
        q = z[:, 0:GLA_KW] * (GLA_DK ** -0.5)
        k = z[:, GLA_KW:2 * GLA_KW]
        v = z[:, 2 * GLA_KW:2 * GLA_KW + GLA_VW]
        r = z[:, 2 * GLA_KW + GLA_VW:2 * GLA_KW + 2 * GLA_VW]
        zz = _dot(zg_ref[i].astype(BF16), wg_ref[...]) + bg_ref[...]
        loga = _log_sigmoid(zz) * (1.0 / GLA_TAU)
        if n_valid < L:
            loga = jnp.where(valid, loga, 0.0)
            k = jnp.where(valid, k, 0.0)
        bc = _dot(tri, loga, HI)
        S = s_s[i]
        vb = v.astype(BF16)
        o_s[...] = _dot((q * jnp.exp(bc)).astype(BF16), S.astype(BF16))
        for j in range(L // SUB - 1):
            r0 = SUB * (j + 1)
            e_j = bc[r0 - 1:r0, :]
            kt = k[r0 - SUB:r0, :] * jnp.exp(e_j - bc[r0 - SUB:r0, :])
            qt = q[r0:, :] * jnp.exp(bc[r0:, :] - e_j)
            kst = jnp.concatenate([jnp.where(lane128 // GLA_DK == h, kt, 0.0) for h in range(GLA_HEADS)], axis=0)
            a = _dot_nt(qt.astype(BF16), kst.astype(BF16))
            vj = v[r0 - SUB:r0, :]
            vst = jnp.concatenate([jnp.where(lane256 // GLA_DV == h, vj, 0.0) for h in range(GLA_HEADS)], axis=0)
            o_s[r0:, :] += _dot(a.astype(BF16), vst.astype(BF16))
        o_diag = jnp.zeros((L, GLA_VW), F32)
        for d in range(SUB):
            qd = q if d == 0 else pltpu.roll(q, L - d, 0)
            bcd = bc if d == 0 else pltpu.roll(bc, L - d, 0)
            ok = (rowi % SUB) + d < SUB
            p = jnp.where(ok, qd * k * jnp.exp(jnp.where(ok, bcd - bc, 0.0)), 0.0)
            u = _dot(p.astype(BF16), segexp) * v
            o_diag = o_diag + (u if d == 0 else pltpu.roll(u, d, 0))
        o = o_s[...] + o_diag
        last = bc[L - 1:L, :]
        kdec = k * jnp.exp(last - bc)
        last_col = bc.T[:, L - 1:L]
        s_s[i] = S * jnp.exp(last_col) + jnp.where(bd, _dot_tn(kdec.astype(BF16), vb), 0.0)
        ms = _dot(o * o, segmean, HI)
        y = o * lax.rsqrt(ms + NORM_EPS) * ng * (r * _sigmoid(r))
        y_ref[i] = y.astype(y_ref.dtype)

    @pl.when(c == nc - 1)
    def _():
        so_ref[...] = s_s[...]


def gla(z3, zg3, wg, bg, s0, ng, n_valid, bb):
    B, S, _ = z3.shape
    nc = S // CHUNK
    return pl.pallas_call(
        functools.partial(_gla_kernel, n_valid, bb),
        out_shape=[jax.ShapeDtypeStruct((B, S, GLA_VW), BF16),
                   jax.ShapeDtypeStruct((B, GLA_KW, GLA_VW), F32)],
        grid=(B // bb, nc),
        in_specs=[pl.BlockSpec((bb, CHUNK, 2 * GLA_KW + 2 * GLA_VW), lambda b, c: (b, c, 0)),
                  pl.BlockSpec((bb, CHUNK, 128), lambda b, c: (b, c, 0)),
                  pl.BlockSpec((128, GLA_KW), lambda b, c: (0, 0)),
                  pl.BlockSpec((1, GLA_KW), lambda b, c: (0, 0)),
                  pl.BlockSpec((bb, GLA_KW, GLA_VW), lambda b, c: (b, 0, 0)),
                  pl.BlockSpec((1, GLA_VW), lambda b, c: (0, 0))],
        out_specs=[pl.BlockSpec((bb, CHUNK, GLA_VW), lambda b, c: (b, c, 0)),
                   pl.BlockSpec((bb, GLA_KW, GLA_VW), lambda b, c: (b, 0, 0))],
        scratch_shapes=[pltpu.VMEM((bb, GLA_KW, GLA_VW), F32), pltpu.VMEM((CHUNK, GLA_VW), F32)],
        compiler_params=_cp("parallel", "arbitrary"),
        name="gla",
    )(z3, zg3, wg, bg, s0, ng)


def _fox_decode_kernel(nps, pt_ref, q_ref, kn_ref, vn_ref, zgn_ref, *refs):
    k_refs = refs[0:nps]
    v_refs = refs[nps:2 * nps]
    lf_refs = refs[2 * nps:3 * nps]
    o_ref, lfo_ref = refs[3 * nps:3 * nps + 2]
    qbd_s, pad_s, m_s, l_s, acc_s, cc_s = refs[3 * nps + 2:]
    st = pl.program_id(1)
    ns = pl.num_programs(1)
    P = PAGE_SIZE
    lane128 = _iota((1, 128), 1)
    lane512 = _iota((1, FOX_W), 1)
    eye8 = _iota((8, 128), 0) == _iota((8, 128), 1)
    head8 = _iota((8, FOX_W), 0) == _iota((8, FOX_W), 1) // FOX_DH

    def to_col(rowvec):
        return jnp.sum(jnp.where(eye8, jnp.broadcast_to(rowvec, (8, 128)), 0.0), axis=1, keepdims=True)

    @pl.when(st == 0)
    def _():
        q = q_ref[...]
        qbt = jnp.where(_iota((128, FOX_W), 0) == _iota((128, FOX_W), 1) // FOX_DH,
                        jnp.broadcast_to(q, (128, FOX_W)), 0.0)
        for t in range(FOX_W // 128):
            qbd_s[t * 128:(t + 1) * 128, :] = qbt[:, t * 128:(t + 1) * 128].T.astype(BF16)
        pad_s[...] = jnp.zeros_like(pad_s)
        m_s[...] = jnp.full_like(m_s, NEG_INF)
        l_s[...] = jnp.zeros_like(l_s)
        acc_s[...] = jnp.zeros_like(acc_s)
        cc_s[...] = jnp.zeros_like(cc_s)

    tri = (_iota((P, P), 0) >= _iota((P, P), 1)).astype(F32)
    qbd = qbd_s[...]
    for i in range(nps):
        pad_s[:, 0:FOX_HEADS] = lf_refs[i][...]
        cs = _dot(tri, pad_s[...], HI) + cc_s[...]
        cc_s[...] = cs[P - 1:P, :]
        s = _dot(k_refs[i][...].astype(BF16), qbd) - cs
        m = m_s[...]
        m_new = jnp.maximum(m, jnp.max(s, axis=0, keepdims=True))
        a = jnp.exp(m - m_new)
        p = jnp.exp(s - m_new)
        l_s[...] = a * l_s[...] + jnp.sum(p, axis=0, keepdims=True)
        m_s[...] = m_new
        pt8 = p.T[0:8, :].astype(BF16)
        acc_s[...] = acc_s[...] * to_col(a) + _dot(pt8, v_refs[i][...].astype(BF16))

    @pl.when(st == ns - 1)
    def _():
        lfn = _log_sigmoid(zgn_ref[...])
        lfo_ref[...] = lfn
        kn = jnp.broadcast_to(kn_ref[...], (8, FOX_W)).astype(BF16)
        s_n = _dot(kn, qbd)[0:1, :] - (cc_s[...] + lfn)
        m = m_s[...]
        m_new = jnp.maximum(m, s_n)
        a = jnp.exp(m - m_new)
        p_n = jnp.exp(s_n - m_new)
        l = a * l_s[...] + p_n
        acc = acc_s[...] * to_col(a) + to_col(p_n) * vn_ref[...]
        o8 = jnp.where(head8, acc / to_col(l), 0.0)
        o_ref[...] = jnp.sum(o8, axis=0, keepdims=True).astype(o_ref.dtype)


def fox_decode(page_table, q3, kn3, vn3, zgn3, pool_k, pool_v, pool_lf, layer, n_pool, nps):
    DB, n_pages = page_table.shape
    base = layer * n_pool
    row = lambda n: pl.BlockSpec((None, 1, n), lambda b, s, pt: (b, 0, 0))

    def page_spec(i, n):
        return pl.BlockSpec((None, PAGE_SIZE, n), lambda b, s, pt: (base + pt[b, s * nps + i], 0, 0))

    in_specs = ([row(FOX_W), row(FOX_W), row(FOX_W), row(128)]
                + [page_spec(i, FOX_W) for i in range(nps)]
                + [page_spec(i, FOX_W) for i in range(nps)]
                + [page_spec(i, FOX_HEADS) for i in range(nps)])
    grid_spec = pltpu.PrefetchScalarGridSpec(
        num_scalar_prefetch=1,
        grid=(DB, n_pages // nps),
        in_specs=in_specs,
        out_specs=[row(FOX_W), row(128)],
        scratch_shapes=[pltpu.VMEM((FOX_W, 128), BF16), pltpu.VMEM((PAGE_SIZE, 128), F32),
                        pltpu.VMEM((1, 128), F32), pltpu.VMEM((1, 128), F32),
                        pltpu.VMEM((8, FOX_W), F32), pltpu.VMEM((1, 128), F32)])
    return pl.pallas_call(
        functools.partial(_fox_decode_kernel, nps),
        out_shape=[jax.ShapeDtypeStruct((DB, 1, FOX_W), BF16), jax.ShapeDtypeStruct((DB, 1, 128), F32)],
        grid_spec=grid_spec,
        compiler_params=_cp("parallel", "arbitrary"),
        name="fox_decode",
    )(page_table, q3, kn3, vn3, zgn3, *([pool_k] * nps), *([pool_v] * nps), *([pool_lf] * nps))


def _outproj_kernel(yml_ref, ygla_ref, yfox_ref, h_ref, w_ref, g_ref, b_ref, wr_ref, br_ref, h1_ref, lg_ref):
    mix = (_dot(yml_ref[...], w_ref[0:ML_W, :]) + _dot(ygla_ref[...], w_ref[ML_W:ML_W + GLA_VW, :])
           + _dot(yfox_ref[...], w_ref[ML_W + GLA_VW:, :]))
    h1 = _ln_rows(DN_ALPHA * h_ref[...] + mix, g_ref[...], b_ref[...])
    h1_ref[...] = h1
    lg_ref[...] = _dot(h1.astype(BF16), wr_ref[...]) + br_ref[...]


def outproj(yml, ygla, yfox, h, w, g, b, wr, br, tm):
    T = h.shape[0]
    tok = lambda n: pl.BlockSpec((tm, n), lambda i: (i, 0))
    full = lambda r, n: pl.BlockSpec((r, n), lambda i: (0, 0))
    return pl.pallas_call(
        _outproj_kernel,
        out_shape=[jax.ShapeDtypeStruct((T, D_MODEL), F32), jax.ShapeDtypeStruct((T, 128), F32)],
        grid=(T // tm,),
        in_specs=[tok(ML_W), tok(GLA_VW), tok(FOX_W), tok(D_MODEL), full(D_MODEL, D_MODEL),
                  full(1, D_MODEL), full(1, D_MODEL), full(D_MODEL, 128), full(1, 128)],
        out_specs=[tok(D_MODEL), tok(128)],
        compiler_params=_cp("parallel"),
        name="outproj",
    )(yml, ygla, yfox, h, w, g, b, wr, br)


def _route_kernel(lg_ref, comb_ref):
    tm = lg_ref.shape[0]
    lt = lg_ref[...].T
    g = lt[_R_GRP:_R_GRP + N_GROUPS, :]
    e = lt[_R_EXP:_R_EXP + N_EXPERTS, :]
    gmax = jnp.max(g, axis=0, keepdims=True)
    gidx = _iota((N_GROUPS, tm), 0)
    g_sel = jnp.min(jnp.where(g == gmax, gidx, N_GROUPS), axis=0, keepdims=True)
    p_grp = 1.0 / jnp.sum(jnp.exp(g - gmax), axis=0, keepdims=True)
    eidx = _iota((N_EXPERTS, tm), 0)
    ev = jnp.where(eidx // EXPERTS_PER_GROUP == g_sel, e, NEG_INF)
    v1 = jnp.max(ev, axis=0, keepdims=True)
    i1 = jnp.min(jnp.where(ev == v1, eidx, N_EXPERTS), axis=0, keepdims=True)
    ev2 = jnp.where(eidx == i1, NEG_INF, ev)
    v2 = jnp.max(ev2, axis=0, keepdims=True)
    i2 = jnp.min(jnp.where(ev2 == v2, eidx, N_EXPERTS), axis=0, keepdims=True)
    t = jnp.exp(v2 - v1)
    w1 = p_grp / (1.0 + t)
    w2 = p_grp * t / (1.0 + t)
    comb = jnp.where(eidx == i1, w1, 0.0) + jnp.where(eidx == i2, w2, 0.0)
    comb128 = jnp.concatenate([comb, jnp.zeros((128 - N_EXPERTS, tm), F32)], axis=0)
    comb_ref[...] = comb128.T


def route(lg, tm):
    T = lg.shape[0]
    return pl.pallas_call(
        _route_kernel,
        out_shape=jax.ShapeDtypeStruct((T, 128), F32),
        grid=(T // tm,),
        in_specs=[pl.BlockSpec((tm, 128), lambda i: (i, 0))],
        out_specs=pl.BlockSpec((tm, 128), lambda i: (i, 0)),
        compiler_params=_cp("parallel"),
        name="route",
    )(lg)


def _moe_kernel(x_ref, comb_ref, wg_ref, wu_ref, wd_ref, g_ref, b_ref, o_ref, acc_s, xb_s):
    e = pl.program_id(1)

    @pl.when(e == 0)
    def _():
        acc_s[...] = jnp.zeros_like(acc_s)
        xb_s[...] = x_ref[...].astype(BF16)

    xb = xb_s[...]
    a = _dot(xb, wg_ref[...])
    u = _dot(xb, wu_ref[...])
    lane = _iota((1, 128), 1)
    ce = jnp.sum(jnp.where(lane == e, comb_ref[...], 0.0), axis=1, keepdims=True)
    hdn = (a * _sigmoid(a)) * u * ce
    acc_s[...] += _dot(hdn.astype(BF16), wd_ref[...])

    @pl.when(e == pl.num_programs(1) - 1)
    def _():
        o_ref[...] = _ln_rows(DN_ALPHA * x_ref[...] + acc_s[...], g_ref[...], b_ref[...])


def moe(x, comb, wg, wu, wd, g, b, tm):
    T = x.shape[0]
    return pl.pallas_call(
        _moe_kernel,
        out_shape=jax.ShapeDtypeStruct((T, D_MODEL), F32),
        grid=(T // tm, N_EXPERTS),
        in_specs=[pl.BlockSpec((tm, D_MODEL), lambda i, e: (i, 0)),
                  pl.BlockSpec((tm, 128), lambda i, e: (i, 0)),
                  pl.BlockSpec((None, D_MODEL, D_EXPERT), lambda i, e: (e, 0, 0)),
                  pl.BlockSpec((None, D_MODEL, D_EXPERT), lambda i, e: (e, 0, 0)),
                  pl.BlockSpec((None, D_EXPERT, D_MODEL), lambda i, e: (e, 0, 0)),
                  pl.BlockSpec((1, D_MODEL), lambda i, e: (0, 0)),
                  pl.BlockSpec((1, D_MODEL), lambda i, e: (0, 0))],
        out_specs=pl.BlockSpec((tm, D_MODEL), lambda i, e: (i, 0)),
        scratch_shapes=[pltpu.VMEM((tm, D_MODEL), F32), pltpu.VMEM((tm, D_MODEL), BF16)],
        compiler_params=_cp("parallel", "arbitrary"),
        name="moe",
    )(x, comb, wg, wu, wd, g, b)


def _perm_index():
    idx = []
    for name, n in (("ml_q", 256), ("ml_k", 256), ("ml_v", 256), ("ml_o", 256),
                    ("g_q", 128), ("g_k", 128), ("g_v", 256), ("g_r", 256),
                    ("fx_q", 512), ("fx_k", 512), ("fx_v", 512),
                    ("fx_f", 8), ("ml_i", 4), ("ml_f", 4), ("g_lr", 16)):
        idx.extend(range(_OFF[name], _OFF[name] + n))
    return np.asarray(idx, np.int32)


def _col_scale():
    s = np.ones((_NPAD,), np.float32)
    s[_ML0 + ML_W:_ML0 + 2 * ML_W] = ML_DH ** -0.5
    s[_FQ0:_FK0] = FOX_DH ** -0.5
    return s


def _prep_inproj(w_in, b_in):
    idx = _perm_index()
    pad = _NPAD - idx.shape[0]
    scale = jnp.asarray(_col_scale())
    w = jnp.pad(jnp.take(w_in, idx, axis=2), ((0, 0), (0, 0), (0, pad))) * scale
    b = jnp.pad(jnp.take(b_in, idx, axis=1), ((0, 0), (0, pad))) * scale
    return w.astype(BF16), b.reshape(DEPTH, 1, _NPAD)


def _block_diag_tile(x, reps, blk_r, blk_c):
    B, R, _ = x.shape
    t = jnp.tile(x, (1, 1, reps))
    keep = (np.arange(R)[:, None] // blk_r) == (np.arange(reps * blk_c)[None, :] // blk_c)
    return jnp.where(jnp.asarray(keep), t, 0.0)


def _diag_blocks(x, heads, blk_r, blk_c):
    B = x.shape[0]
    x5 = x.reshape(B, heads, blk_r, heads, blk_c)
    return jnp.stack([x5[:, h, :, h, :] for h in range(heads)], axis=1)


def _pad_chunk(x):
    return jnp.pad(x[:, None, :], ((0, 0), (0, CHUNK - 1), (0, 0)))


def kernel(x_prompt, x_sample, cache_fox_k, cache_fox_v, cache_fox_logf, state_mlstm_C, state_mlstm_n, state_mlstm_m, state_gla_S, page_table, ln_in_g, ln_in_b, w_in, b_in, mlstm_norm_g, gla_w_gate_up, gla_b_gate, gla_norm_g, w_out, ln1_g, ln1_b, w_router_group, b_router_group, w_router_expert, b_router_expert, w_exp_gate, w_exp_up, w_exp_down, ln2_g, ln2_b):
    B, S, D = x_prompt.shape
    DB = x_sample.shape[0]
    T = B * S
    n_pool = cache_fox_k.shape[1]
    TS = 128

    w_in_p, b_in_p = _prep_inproj(w_in, b_in)
    w_out_b = w_out.astype(BF16)
    wr = jnp.zeros((DEPTH, D, 128), F32)
    wr = wr.at[:, :, _R_GRP:_R_GRP + N_GROUPS].set(w_router_group).at[:, :, _R_EXP:_R_EXP + N_EXPERTS].set(w_router_expert)
    wr = wr.astype(BF16)
    br = jnp.zeros((DEPTH, 1, 128), F32)
    br = br.at[:, 0, _R_GRP:_R_GRP + N_GROUPS].set(b_router_group).at[:, 0, _R_EXP:_R_EXP + N_EXPERTS].set(b_router_expert)
    wg_b = w_exp_gate.astype(BF16)
    wu_b = w_exp_up.astype(BF16)
    wd_b = w_exp_down.astype(BF16)
    wgl = jnp.zeros((DEPTH, 128, GLA_KW), F32).at[:, _G_LR:_G_LR + GLA_RANK, :].set(gla_w_gate_up).astype(BF16)
    row = lambda a: a.reshape(DEPTH, 1, -1)
    ln1g, ln1b, ln2g, ln2b = row(ln1_g), row(ln1_b), row(ln2_g), row(ln2_b)
    mlng, glng, glbg = row(mlstm_norm_g), row(gla_norm_g), row(gla_b_gate)
    pool_k = cache_fox_k.reshape(DEPTH * n_pool, PAGE_SIZE, FOX_W)
    pool_v = cache_fox_v.reshape(DEPTH * n_pool, PAGE_SIZE, FOX_W)
    pool_lf = cache_fox_logf.reshape(DEPTH * n_pool, PAGE_SIZE, FOX_HEADS)

    g_in, b_in_ln = ln_in_g.reshape(1, D), ln_in_b.reshape(1, D)
    hp = ln_rows(x_prompt.reshape(T, D), g_in, b_in_ln, 512)
    xs = jnp.pad(x_sample.reshape(DB, D), ((0, TS - DB), (0, 0)))
    hs = ln_rows(xs, g_in, b_in_ln, TS)

    zc = jnp.zeros((B, ML_W, ML_W), F32)
    zn = jnp.zeros((B, 1, ML_W), F32)
    zm = jnp.zeros((B, 1, 128), F32)
    zs = jnp.zeros((B, GLA_KW, GLA_VW), F32)

    outs = {k: [] for k in ("kp", "vp", "lfp", "ks", "vs", "lfs", "cp", "np", "mp", "cs", "ns", "ms", "gp", "gs")}
    for l in range(DEPTH):
        zml, zgla, fq, fk, fv, zg = inproj(hp, w_in_p[l], b_in_p[l], 256)
        zg3 = zg.reshape(B, S, 128)
        lf, c_hs = foxgate(zg3)
        tq = 256
        y_fox = fox_attn(fq, fk, fv, c_hs.reshape(B * (FOX_HEADS // 2), 2, S // tq, tq), B, S, tq)
        y_ml, c_o, n_o, m_o = mlstm(zml.reshape(B, S, 4 * ML_W), zg3, zc, zn, zm, mlng[l], CHUNK, 2)
        y_gla, s_o = gla(zgla.reshape(B, S, -1), zg3, wgl[l], glbg[l], zs, glng[l], CHUNK, 2)
        h1, lg = outproj(y_ml.reshape(T, ML_W), y_gla.reshape(T, GLA_VW), y_fox, hp, w_out_b[l],
                         ln1g[l], ln1b[l], wr[l], br[l], 512)
        comb = route(lg, 512)
        hp = moe(h1, comb, wg_b[l], wu_b[l], wd_b[l], ln2g[l], ln2b[l], min(1024, T))
        outs["kp"].append(fk); outs["vp"].append(fv); outs["lfp"].append(lf)
        outs["cp"].append(_diag_blocks(c_o, ML_HEADS, ML_DH, ML_DH))
        outs["np"].append(n_o.reshape(B, ML_HEADS, ML_DH)); outs["mp"].append(m_o[:, 0, :ML_HEADS])
        outs["gp"].append(_diag_blocks(s_o, GLA_HEADS, GLA_DK, GLA_DV))

        zml, zgla, fq, fk, fv, zg = inproj(hs, w_in_p[l], b_in_p[l], TS)
        zg3 = _pad_chunk(zg[:DB])
        c0 = _block_diag_tile(state_mlstm_C[l].astype(F32).reshape(DB, ML_W, ML_DH), ML_HEADS, ML_DH, ML_DH)
        n0 = state_mlstm_n[l].astype(F32).reshape(DB, 1, ML_W)
        m0 = jnp.pad(state_mlstm_m[l].astype(F32), ((0, 0), (0, 128 - ML_HEADS))).reshape(DB, 1, 128)
        s0 = _block_diag_tile(state_gla_S[l].astype(F32).reshape(DB, GLA_KW, GLA_DV), GLA_HEADS, GLA_DK, GLA_DV)
        y_ml, c_o, n_o, m_o = mlstm(_pad_chunk(zml[:DB]), zg3, c0, n0, m0, mlng[l], 1, 4)
        y_gla, s_o = gla(_pad_chunk(zgla[:DB]), zg3, wgl[l], glbg[l], s0, glng[l], 1, 4)
        y_fox, lfn = fox_decode(page_table, fq[:DB].astype(F32).reshape(DB, 1, FOX_W),
                                fk[:DB].reshape(DB, 1, FOX_W), fv[:DB].reshape(DB, 1, FOX_W),
                                zg[:DB].reshape(DB, 1, 128), pool_k, pool_v, pool_lf, l, n_pool, 8)
        padr = lambda a: jnp.pad(a, ((0, TS - DB), (0, 0)))
        h1, lg = outproj(padr(y_ml[:, 0, :]), padr(y_gla[:, 0, :]), padr(y_fox.reshape(DB, FOX_W)), hs, w_out_b[l],
                         ln1g[l], ln1b[l], wr[l], br[l], TS)
        comb = route(lg, TS)
        hs = moe(h1, comb, wg_b[l], wu_b[l], wd_b[l], ln2g[l], ln2b[l], TS)
        outs["ks"].append(fk[:DB]); outs["vs"].append(fv[:DB]); outs["lfs"].append(lfn[:, 0, :FOX_HEADS])
        outs["cs"].append(_diag_blocks(c_o, ML_HEADS, ML_DH, ML_DH))
        outs["ns"].append(n_o.reshape(DB, ML_HEADS, ML_DH)); outs["ms"].append(m_o[:, 0, :ML_HEADS])
        outs["gs"].append(_diag_blocks(s_o, GLA_HEADS, GLA_DK, GLA_DV))

    st = lambda k, shape, dt: jnp.stack(outs[k]).reshape((DEPTH,) + shape).astype(dt)
    kd, vd, ld = cache_fox_k.dtype, cache_fox_v.dtype, cache_fox_logf.dtype
    cd, nd, md, sd = state_mlstm_C.dtype, state_mlstm_n.dtype, state_mlstm_m.dtype, state_gla_S.dtype
    return (hp.reshape(B, S, D), hs[:DB].reshape(DB, 1, D),
            st("kp", (B, S, FOX_HEADS, FOX_DH), kd), st("vp", (B, S, FOX_HEADS, FOX_DH), vd),
            st("lfp", (B, S, FOX_HEADS), ld),
            st("ks", (DB, 1, FOX_HEADS, FOX_DH), kd), st("vs", (DB, 1, FOX_HEADS, FOX_DH), vd),
            st("lfs", (DB, 1, FOX_HEADS), ld),
            st("cp", (B, ML_HEADS, ML_DH, ML_DH), cd), st("np", (B, ML_HEADS, ML_DH), nd), st("mp", (B, ML_HEADS), md),
            st("cs", (DB, ML_HEADS, ML_DH, ML_DH), cd), st("ns", (DB, ML_HEADS, ML_DH), nd), st("ms", (DB, ML_HEADS), md),
            st("gp", (B, GLA_HEADS, GLA_DK, GLA_DV), sd), st("gs", (DB, GLA_HEADS, GLA_DK, GLA_DV), sd))
```

```python
import functools

import numpy as np
import jax
import jax.numpy as jnp
from jax import lax
from jax.experimental import pallas as pl
from jax.experimental.pallas import tpu as pltpu

F32 = jnp.float32
BF16 = jnp.bfloat16
HI = lax.Precision.HIGHEST

D_MODEL = 1024
DEPTH = 4
PAGE_SIZE = 128
ML_DH = 64
ML_HEADS = 4
ML_W = 256
GLA_DK = 32
GLA_DV = 64
GLA_HEADS = 4
GLA_KW = 128
GLA_VW = 256
GLA_RANK = 16
GLA_TAU = 16.0
FOX_DH = 64
FOX_HEADS = 8
FOX_W = 512
CHUNK = 128
SUB = 16
N_GROUPS = 4
EXPERTS_PER_GROUP = 4
N_EXPERTS = 16
D_EXPERT = 256
DN_ALPHA = (2.0 * DEPTH) ** 0.25
LN_EPS = 1e-5
NORM_EPS = 1e-6
NEG_INF = float("-inf")

_OFF = dict(ml_q=0, ml_k=256, ml_v=512, ml_o=768, ml_i=1024, ml_f=1028, g_q=1032, g_k=1160, g_v=1288,
            g_r=1544, g_lr=1800, fx_q=1816, fx_k=2328, fx_v=2840, fx_f=3352)
_ML0, _GLA0, _FQ0, _FK0, _FV0, _ZG0, _NPAD = 0, 1024, 1792, 2304, 2816, 3328, 3456
_G_FXF, _G_MLI, _G_MLF, _G_LR = 0, 8, 12, 16
_R_GRP, _R_EXP = 0, 8

VMEM_LIMIT = 56 * 1024 * 1024
FOX_TQ = 512
FOX_KS, FOX_QS = 128, 256
FOX_LOOKAHEAD = 8
MLSTM_BB, GLA_BB = 2, 4


def _cp(*sem):
    return pltpu.CompilerParams(dimension_semantics=sem, vmem_limit_bytes=VMEM_LIMIT)


def _dot(a, b, precision=None):
    return jnp.dot(a, b, preferred_element_type=F32, precision=precision)


def _dot_nt(a, b):
    return lax.dot_general(a, b, (((1,), (1,)), ((), ())), preferred_element_type=F32)


def _dot_tn(a, b):
    return lax.dot_general(a, b, (((0,), (0,)), ((), ())), preferred_element_type=F32)


def _log_sigmoid(x):
    return jnp.minimum(x, 0.0) - jnp.log1p(jnp.exp(-jnp.abs(x)))


def _sigmoid(x):
    return 1.0 / (1.0 + jnp.exp(-x))


def _ln_rows(x, g, b):
    mu = jnp.mean(x, axis=-1, keepdims=True)
    xc = x - mu
    var = jnp.mean(xc * xc, axis=-1, keepdims=True)
    return xc * lax.rsqrt(var + LN_EPS) * g + b


def _iota(shape, dim):
    return lax.broadcasted_iota(jnp.int32, shape, dim)


def _ln_kernel(x_ref, g_ref, b_ref, o_ref):
    o_ref[...] = _ln_rows(x_ref[...], g_ref[...], b_ref[...])


def ln_rows(x, g, b, tm):
    T, D = x.shape
    return pl.pallas_call(
        _ln_kernel,
        out_shape=jax.ShapeDtypeStruct((T, D), F32),
        grid=(T // tm,),
        in_specs=[pl.BlockSpec((tm, D), lambda i: (i, 0)),
                  pl.BlockSpec((1, D), lambda i: (0, 0)),
                  pl.BlockSpec((1, D), lambda i: (0, 0))],
        out_specs=pl.BlockSpec((tm, D), lambda i: (i, 0)),
        compiler_params=_cp("parallel"),
        name="ln_in",
    )(x, g, b)


_IN_BOUNDS = ((_ML0, _GLA0), (_GLA0, _FQ0), (_FQ0, _FK0), (_FK0, _FV0), (_FV0, _ZG0), (_ZG0, _NPAD))


def _inproj_kernel(h_ref, w_ref, b_ref, *out_refs):
    xb = h_ref[...].astype(BF16)
    for ref, (a, b) in zip(out_refs, _IN_BOUNDS):
        ref[...] = (_dot(xb, w_ref[:, a:b]) + b_ref[:, a:b]).astype(ref.dtype)


def inproj(h, w, b, tm):
    T = h.shape[0]
    widths = [b_ - a_ for a_, b_ in _IN_BOUNDS]
    dtypes = [F32, F32, BF16, F32, F32, F32]
    return pl.pallas_call(
        _inproj_kernel,
        out_shape=[jax.ShapeDtypeStruct((T, n), dt) for n, dt in zip(widths, dtypes)],
        grid=(T // tm,),
        in_specs=[pl.BlockSpec((tm, D_MODEL), lambda i: (i, 0)),
                  pl.BlockSpec((D_MODEL, _NPAD), lambda i: (0, 0)),
                  pl.BlockSpec((1, _NPAD), lambda i: (0, 0))],
        out_specs=[pl.BlockSpec((tm, n), lambda i: (i, 0)) for n in widths],
        compiler_params=_cp("parallel"),
        name="inproj",
    )(h, w, b)


def _inproj_t_kernel(h_ref, w_ref, b_ref, wt_ref, bt_ref, *refs):
    ml_ref, gla_ref, zg_ref, krm_ref, qT_ref, kT_ref, vT_ref = refs[-7:]
    xb = h_ref[...].astype(BF16)
    for ref, (a, b) in ((ml_ref, (_ML0, _GLA0)), (gla_ref, (_GLA0, _FQ0)), (zg_ref, (_ZG0, _NPAD)),
                        (krm_ref, (_FK0, _FV0))):
        ref[...] = (_dot(xb, w_ref[:, a:b]) + b_ref[:, a:b]).astype(ref.dtype)
    for t, ref in enumerate((qT_ref, kT_ref, vT_ref)):
        sl = slice(t * FOX_W, (t + 1) * FOX_W)
        ref[...] = (_dot_nt(wt_ref[sl, :], xb) + bt_ref[sl, :]).astype(ref.dtype)


def inproj_t(h, w, b, wt, bt, kv_all, layer, B, S, tm):
    T = h.shape[0]
    nb = S // tm
    tok = lambda n: pl.BlockSpec((tm, n), lambda i: (i, 0))
    full = lambda r, n: pl.BlockSpec((r, n), lambda i: (0, 0))
    kv_spec = pl.BlockSpec((None, None, FOX_W, tm), lambda i: (layer, i // nb, 0, i % nb))
    kv_shape = jax.ShapeDtypeStruct((DEPTH, B, FOX_W, S), F32)
    in_specs = [tok(D_MODEL), full(D_MODEL, _NPAD), full(1, _NPAD), full(3 * FOX_W, D_MODEL), full(3 * FOX_W, 1)]
    args = [h, w, b, wt, bt]
    aliases = {}
    if kv_all is not None:
        in_specs += [pl.BlockSpec(memory_space=pl.ANY), pl.BlockSpec(memory_space=pl.ANY)]
        args += list(kv_all)
        aliases = {5: 5, 6: 6}
    return pl.pallas_call(
        _inproj_t_kernel,
        out_shape=[jax.ShapeDtypeStruct((T, 4 * ML_W), F32), jax.ShapeDtypeStruct((T, _FQ0 - _GLA0), F32),
                   jax.ShapeDtypeStruct((T, 128), F32), jax.ShapeDtypeStruct((T, FOX_W), BF16),
                   jax.ShapeDtypeStruct((B, FOX_W, S), BF16), kv_shape, kv_shape],
        grid=(T // tm,),
        in_specs=in_specs,
        out_specs=[tok(4 * ML_W), tok(_FQ0 - _GLA0), tok(128), tok(FOX_W),
                   pl.BlockSpec((None, FOX_W, tm), lambda i: (i // nb, 0, i % nb)), kv_spec, kv_spec],
        input_output_aliases=aliases,
        compiler_params=_cp("arbitrary"),
        name="inproj_t",
    )(*args)


def _foxgate_kernel(zg_ref, lft_ref, c_ref):
    S = zg_ref.shape[0]
    tri = (_iota((CHUNK, CHUNK), 0) >= _iota((CHUNK, CHUNK), 1)).astype(F32)
    carry = jnp.zeros((1, 128), F32)
    for j in range(S // CHUNK):
        sl = slice(j * CHUNK, (j + 1) * CHUNK)
        lf = _log_sigmoid(zg_ref[sl, :])
        lft_ref[:, sl] = lf.T[_G_FXF:_G_FXF + FOX_HEADS, :]
        cs = _dot(tri, lf, HI) + carry
        carry = cs[CHUNK - 1:CHUNK, :]
        c_ref[sl, :] = cs


def foxgate(zg3):
    B, S, _ = zg3.shape
    return pl.pallas_call(
        _foxgate_kernel,
        out_shape=[jax.ShapeDtypeStruct((B, FOX_HEADS, S), F32), jax.ShapeDtypeStruct((B, S, 128), F32)],
        grid=(B,),
        in_specs=[pl.BlockSpec((None, S, 128), lambda b: (b, 0, 0))],
        out_specs=[pl.BlockSpec((None, FOX_HEADS, S), lambda b: (b, 0, 0)),
                   pl.BlockSpec((None, S, 128), lambda b: (b, 0, 0))],
        compiler_params=_cp("parallel"),
        name="foxgate",
    )(zg3)


def _fox_attn_kernel(tq, qT_ref, k_ref, vT_ref, c_ref, o_ref, vb_s, c_s):
    hp = pl.program_id(1)
    qi = pl.program_id(2)

    @pl.when(qi == 0)
    def _():
        vb_s[...] = vT_ref[...].astype(BF16)
        sel = (_iota((128, 128), 0) == 2 * hp + _iota((128, 128), 1)).astype(F32)
        c_s[...] = _dot(c_ref[...], sel, HI)

    qT = qT_ref[...]
    rowp = _iota((128, 1), 0)
    qTh = (jnp.where(rowp < FOX_DH, qT, jnp.zeros_like(qT)), jnp.where(rowp >= FOX_DH, qT, jnp.zeros_like(qT)))
    ks, qs = min(FOX_KS, tq), min(FOX_QS, tq)
    nqs = tq // qs
    tri_mask = _iota((ks, qs), 0) - _iota((ks, qs), 1)

    def block(j, carry, masked):
        off = pl.multiple_of(j * tq, tq)
        carry = list(carry)
        tiles = [(kk, hh, t) for kk in range(tq // ks) for hh in range(2) for t in range(nqs)
                 if not (masked and kk * ks >= (t + 1) * qs)]

        def scores(kk, hh, t):
            ko = pl.multiple_of(off + kk * ks, ks)
            s = (_dot(k_ref[pl.ds(ko, ks), :], qTh[hh][:, t * qs:(t + 1) * qs])
                 - c_s[pl.ds(ko, ks), :][:, hh:hh + 1])
            if masked and (kk + 1) * ks - 1 > t * qs:
                s = jnp.where(tri_mask <= t * qs - kk * ks, s, NEG_INF)
            return s

        ready = [scores(*tl) for tl in tiles[:FOX_LOOKAHEAD]]
        for i, (kk, hh, t) in enumerate(tiles):
            if i + FOX_LOOKAHEAD < len(tiles):
                ready.append(scores(*tiles[i + FOX_LOOKAHEAD]))
            s = ready[i]
            ko = pl.multiple_of(off + kk * ks, ks)
            m, l, acc = carry[hh * nqs + t]
            m_new = jnp.maximum(m, jnp.max(s, axis=0, keepdims=True))
            a = jnp.exp(m - m_new)
            p = jnp.exp(s - m_new)
            l = a * l + jnp.sum(p, axis=0, keepdims=True)
            vb = vb_s[hh * FOX_DH:(hh + 1) * FOX_DH, pl.ds(ko, ks)]
            acc = a * acc + _dot(vb, p.astype(BF16))
            carry[hh * nqs + t] = (m_new, l, acc)
        return tuple(carry)

    init = tuple((jnp.full((1, qs), NEG_INF, F32), jnp.zeros((1, qs), F32), jnp.zeros((FOX_DH, qs), F32))
                 for _ in range(2 * nqs))
    carry = lax.fori_loop(0, qi, functools.partial(block, masked=False), init)
    carry = block(qi, carry, True)
    oT = jnp.concatenate(
        [jnp.concatenate([carry[hh * nqs + t][2] / carry[hh * nqs + t][1] for t in range(nqs)], axis=1)
         for hh in range(2)], axis=0)
    o_ref[...] = oT.T.astype(o_ref.dtype)


def fox_attn(qT, k_rm, vT_all, c_sh, layer, B, S, tq):
    T = B * S
    nq = S // tq
    return pl.pallas_call(
        functools.partial(_fox_attn_kernel, tq),
        out_shape=jax.ShapeDtypeStruct((T, FOX_W), BF16),
        grid=(B, FOX_HEADS // 2, nq),
        in_specs=[pl.BlockSpec((None, 128, tq), lambda b, hp, qi: (b, hp, qi)),
                  pl.BlockSpec((S, 128), lambda b, hp, qi: (b, hp)),
                  pl.BlockSpec((None, None, 128, S), lambda b, hp, qi: (layer, b, hp, 0)),
                  pl.BlockSpec((None, S, 128), lambda b, hp, qi: (b, 0, 0))],
        out_specs=pl.BlockSpec((tq, 128), lambda b, hp, qi: (b * nq + qi, hp)),
        scratch_shapes=[pltpu.VMEM((128, S), BF16), pltpu.VMEM((S, 128), F32)],
        compiler_params=_cp("parallel", "arbitrary", "arbitrary"),
        name="fox_attn",
    )(qT, k_rm, vT_all, c_sh)


def _mlstm_kernel(n_valid, bb, z_ref, zg_ref, c0_ref, n0_ref, m0_ref, ng_ref,
                  y_ref, co_ref, no_ref, mo_ref, c_s, n_s, m_s):
    c = pl.program_id(1)
    nc = pl.num_programs(1)
    L = CHUNK

    @pl.when(c == 0)
    def _():
        c_s[...] = c0_ref[...]
        n_s[...] = n0_ref[...]
        m_s[...] = m0_ref[...]

    row = _iota((L, L), 0)
    col = _iota((L, L), 1)
    causal = row >= col
    tri = causal.astype(F32)
    lane256 = _iota((1, ML_W), 1)
    lane128 = _iota((1, 128), 1)
    valid = _iota((L, 1), 0) < n_valid
    bd = _iota((ML_W, ML_W), 0) // ML_DH == _iota((ML_W, ML_W), 1) // ML_DH
    seg = (_iota((ML_W, 128), 0) // ML_DH == _iota((ML_W, 128), 1)).astype(F32)
    segmean = bd.astype(F32) * (1.0 / ML_DH)
    ng = ng_ref[...]

    for i in range(bb):
        z = z_ref[i]
        q = z[:, 0:ML_W]
        k = z[:, ML_W:2 * ML_W]
        v = z[:, 2 * ML_W:3 * ML_W]
        og = z[:, 3 * ML_W:4 * ML_W]
        g = zg_ref[i]
        li_all = g
        lf_all = _log_sigmoid(g)
        if n_valid < L:
            li_all = jnp.where(valid, li_all, NEG_INF)
            lf_all = jnp.where(valid, lf_all, 0.0)
        b_all = _dot(tri, lf_all, HI)
        liT = li_all.T
        bT = b_all.T
        qb = q.astype(BF16)
        kb = k.astype(BF16)
        vb = v.astype(BF16)
        C = c_s[i]
        n_row = n_s[i]
        m_row = m_s[i]
        qC = _dot(qb, C.astype(BF16))
        qn = _dot(q * n_row, seg, HI)
        h_acc = jnp.zeros((L, ML_W), F32)
        kg = jnp.zeros((L, ML_W), F32)
        gp_row = jnp.zeros((1, ML_W), F32)
        m_new_row = m_row
        for h in range(ML_HEADS):
            hm = lane256 // ML_DH == h
            b_col = b_all[:, _G_MLF + h:_G_MLF + h + 1]
            b_row = bT[_G_MLF + h:_G_MLF + h + 1, :]
            li_row = liT[_G_MLI + h:_G_MLI + h + 1, :]
            li_col = li_all[:, _G_MLI + h:_G_MLI + h + 1]
            m_prev = m_row[:, h:h + 1]
            dmat = jnp.where(causal, b_col - b_row + li_row, NEG_INF)
            inter = b_col + m_prev
            m_t = jnp.maximum(inter, jnp.max(dmat, axis=1, keepdims=True))
            w_prev = jnp.exp(inter - m_t)
            e = jnp.exp(dmat - m_t)
            qh = jnp.where(hm, q, 0.0).astype(BF16)
            s = _dot_nt(qh, kb) * e
            num = _dot(s.astype(BF16), vb)
            den = jnp.sum(s, axis=1, keepdims=True) + w_prev * qn[:, h:h + 1]
            den = jnp.maximum(jnp.abs(den), jnp.exp(-m_t))
            hfull = (num + qC * w_prev) / den
            h_acc = jnp.where(hm, hfull, h_acc)
            b_last = b_col[L - 1:L, :]
            m_new = m_t[L - 1:L, :]
            g_prev = jnp.exp(b_last + m_prev - m_new)
            g_rows = jnp.exp(b_last - b_col + li_col - m_new)
            kg = jnp.where(hm, k * g_rows, kg)
            gp_row = jnp.where(hm, g_prev, gp_row)
            m_new_row = jnp.where(lane128 == h, m_new, m_new_row)
        upd = _dot_tn(kg.astype(BF16), vb)
        c_s[i] = C * gp_row + jnp.where(bd, upd, 0.0)
        n_s[i] = n_row * gp_row + jnp.sum(kg, axis=0, keepdims=True)
        m_s[i] = m_new_row
        ms = _dot(h_acc * h_acc, segmean, HI)
        y = h_acc * lax.rsqrt(ms + NORM_EPS) * ng * _sigmoid(og)
        y_ref[i] = y.astype(y_ref.dtype)

    @pl.when(c == nc - 1)
    def _():
        co_ref[...] = c_s[...]
        no_ref[...] = n_s[...]
        mo_ref[...] = m_s[...]


def mlstm(z3, zg3, c0, n0, m0, ng, n_valid, bb):
    B, S, _ = z3.shape
    nc = S // CHUNK
    st = lambda shape: pl.BlockSpec((bb,) + shape, lambda b, c: (b, 0, 0))
    return pl.pallas_call(
        functools.partial(_mlstm_kernel, n_valid, bb),
        out_shape=[jax.ShapeDtypeStruct((B, S, ML_W), BF16),
                   jax.ShapeDtypeStruct((B, ML_W, ML_W), F32),
                   jax.ShapeDtypeStruct((B, 1, ML_W), F32),
                   jax.ShapeDtypeStruct((B, 1, 128), F32)],
        grid=(B // bb, nc),
        in_specs=[pl.BlockSpec((bb, CHUNK, 4 * ML_W), lambda b, c: (b, c, 0)),
                  pl.BlockSpec((bb, CHUNK, 128), lambda b, c: (b, c, 0)),
                  st((ML_W, ML_W)), st((1, ML_W)), st((1, 128)),
                  pl.BlockSpec((1, ML_W), lambda b, c: (0, 0))],
        out_specs=[pl.BlockSpec((bb, CHUNK, ML_W), lambda b, c: (b, c, 0)),
                   st((ML_W, ML_W)), st((1, ML_W)), st((1, 128))],
        scratch_shapes=[pltpu.VMEM((bb, ML_W, ML_W), F32), pltpu.VMEM((bb, 1, ML_W), F32),
                        pltpu.VMEM((bb, 1, 128), F32)],
        compiler_params=_cp("parallel", "arbitrary"),
        name="mlstm",
    )(z3, zg3, c0, n0, m0, ng)


def _gla_kernel(n_valid, bb, z_ref, zg_ref, wg_ref, bg_ref, s0_ref, ng_ref, y_ref, so_ref, s_s, o_s):
    c = pl.program_id(1)
    nc = pl.num_programs(1)
    L = CHUNK

    @pl.when(c == 0)
    def _():
        s_s[...] = s0_ref[...]

    tri = (_iota((L, L), 0) >= _iota((L, L), 1)).astype(F32)
    lane128 = _iota((1, GLA_KW), 1)
    lane256 = _iota((1, GLA_VW), 1)
    rowi = _iota((L, 1), 0)
    valid = rowi < n_valid
    bd = _iota((GLA_KW, GLA_VW), 0) // GLA_DK == _iota((GLA_KW, GLA_VW), 1) // GLA_DV
    segexp = bd.astype(BF16)
    bdv = _iota((GLA_VW, GLA_VW), 0) // GLA_DV == _iota((GLA_VW, GLA_VW), 1) // GLA_DV
    segmean = bdv.astype(F32) * (1.0 / GLA_DV)
    ng = ng_ref[...]

    for i in range(bb):
        z = z_ref[i]
        q = z[:, 0:GLA_KW] * (GLA_DK ** -0.5)
        k = z[:, GLA_KW:2 * GLA_KW]
        v = z[:, 2 * GLA_KW:2 * GLA_KW + GLA_VW]
        r = z[:, 2 * GLA_KW + GLA_VW:2 * GLA_KW + 2 * GLA_VW]
        zz = _dot(zg_ref[i].astype(BF16), wg_ref[...]) + bg_ref[...]
        loga = _log_sigmoid(zz) * (1.0 / GLA_TAU)
        if n_valid < L:
            loga = jnp.where(valid, loga, 0.0)
            k = jnp.where(valid, k, 0.0)
        bc = _dot(tri, loga, HI)
        S = s_s[i]
        vb = v.astype(BF16)
        o_s[...] = _dot((q * jnp.exp(bc)).astype(BF16), S.astype(BF16))
        for j in range(L // SUB - 1):
            r0 = SUB * (j + 1)
            e_j = bc[r0 - 1:r0, :]
            kt = k[r0 - SUB:r0, :] * jnp.exp(e_j - bc[r0 - SUB:r0, :])
            qt = q[r0:, :] * jnp.exp(bc[r0:, :] - e_j)
            kst = jnp.concatenate([jnp.where(lane128 // GLA_DK == h, kt, 0.0) for h in range(GLA_HEADS)], axis=0)
            a = _dot_nt(qt.astype(BF16), kst.astype(BF16))
            vj = v[r0 - SUB:r0, :]
            vst = jnp.concatenate([jnp.where(lane256 // GLA_DV == h, vj, 0.0) for h in range(GLA_HEADS)], axis=0)
            o_s[r0:, :] += _dot(a.astype(BF16), vst.astype(BF16))
        o_diag = jnp.zeros((L, GLA_VW), F32)
        for d in range(SUB):
            qd = q if d == 0 else pltpu.roll(q, L - d, 0)
            bcd = bc if d == 0 else pltpu.roll(bc, L - d, 0)
            ok = (rowi % SUB) + d < SUB
            p = jnp.where(ok, qd * k * jnp.exp(jnp.where(ok, bcd - bc, 0.0)), 0.0)
            u = _dot(p.astype(BF16), segexp) * v
            o_diag = o_diag + (u if d == 0 else pltpu.roll(u, d, 0))
        o = o_s[...] + o_diag
        last = bc[L - 1:L, :]
        kdec = k * jnp.exp(last - bc)
        last_col = bc.T[:, L - 1:L]
        s_s[i] = S * jnp.exp(last_col) + jnp.where(bd, _dot_tn(kdec.astype(BF16), vb), 0.0)
        ms = _dot(o * o, segmean, HI)
        y = o * lax.rsqrt(ms + NORM_EPS) * ng * (r * _sigmoid(r))
        y_ref[i] = y.astype(y_ref.dtype)

    @pl.when(c == nc - 1)
    def _():
        so_ref[...] = s_s[...]


def gla(z3, zg3, wg, bg, s0, ng, n_valid, bb):
    B, S, _ = z3.shape
    nc = S // CHUNK
    return pl.pallas_call(
        functools.partial(_gla_kernel, n_valid, bb),
        out_shape=[jax.ShapeDtypeStruct((B, S, GLA_VW), BF16),
                   jax.ShapeDtypeStruct((B, GLA_KW, GLA_VW), F32)],
        grid=(B // bb, nc),
        in_specs=[pl.BlockSpec((bb, CHUNK, 2 * GLA_KW + 2 * GLA_VW), lambda b, c: (b, c, 0)),
                  pl.BlockSpec((bb, CHUNK, 128), lambda b, c: (b, c, 0)),
                  pl.BlockSpec((128, GLA_KW), lambda b, c: (0, 0)),
                  pl.BlockSpec((1, GLA_KW), lambda b, c: (0, 0)),
                  pl.BlockSpec((bb, GLA_KW, GLA_VW), lambda b, c: (b, 0, 0)),
                  pl.BlockSpec((1, GLA_VW), lambda b, c: (0, 0))],
        out_specs=[pl.BlockSpec((bb, CHUNK, GLA_VW), lambda b, c: (b, c, 0)),
                   pl.BlockSpec((bb, GLA_KW, GLA_VW), lambda b, c: (b, 0, 0))],
        scratch_shapes=[pltpu.VMEM((bb, GLA_KW, GLA_VW), F32), pltpu.VMEM((CHUNK, GLA_VW), F32)],
        compiler_params=_cp("parallel", "arbitrary"),
        name="gla",
    )(z3, zg3, wg, bg, s0, ng)


def _fox_decode_kernel(nps, pt_ref, q_ref, kn_ref, vn_ref, zgn_ref, *refs):
    k_refs = refs[0:nps]
    v_refs = refs[nps:2 * nps]
    lf_refs = refs[2 * nps:3 * nps]
    o_ref, lfo_ref = refs[3 * nps:3 * nps + 2]
    qbd_s, m_s, l_s, acc_s, cc_s = refs[3 * nps + 2:]
    st = pl.program_id(1)
    ns = pl.num_programs(1)
    P = PAGE_SIZE
    eye8 = _iota((8, 128), 0) == _iota((8, 128), 1)
    head8 = _iota((8, FOX_W), 0) == _iota((8, FOX_W), 1) // FOX_DH

    @pl.when(st == 0)
    def _():
        qbd = jnp.where(_iota((16, FOX_W), 0) == _iota((16, FOX_W), 1) // FOX_DH,
                        jnp.broadcast_to(q_ref[...], (16, FOX_W)), 0.0)
        qbd_s[...] = qbd.astype(BF16)
        m_s[...] = jnp.full_like(m_s, NEG_INF)
        l_s[...] = jnp.zeros_like(l_s)
        acc_s[...] = jnp.zeros_like(acc_s)
        cc_s[...] = jnp.zeros_like(cc_s)

    triu = (_iota((P, P), 0) <= _iota((P, P), 1)).astype(F32)
    qbd = qbd_s[...]
    local = _dot(jnp.concatenate([r[...] for r in lf_refs], axis=0), triu, HI)
    carry = cc_s[...]
    s_parts = []
    for i in range(nps):
        loc = local[8 * i:8 * (i + 1), :]
        s_parts.append(_dot(qbd, k_refs[i][...].astype(BF16))[0:8, :] - (loc + carry))
        carry = carry + loc[:, P - 1:P]
    cc_s[...] = carry
    m_loc = s_parts[0]
    for sp in s_parts[1:]:
        m_loc = jnp.maximum(m_loc, sp)
    m = m_s[...]
    m_new = jnp.maximum(m, jnp.max(m_loc, axis=1, keepdims=True))
    a = jnp.exp(m - m_new)
    m_s[...] = m_new
    p_parts = [jnp.exp(sp - m_new) for sp in s_parts]
    p_sum = p_parts[0]
    for pp in p_parts[1:]:
        p_sum = p_sum + pp
    l_s[...] = a * l_s[...] + jnp.sum(p_sum, axis=1, keepdims=True)
    o_even = jnp.zeros((16, FOX_W), F32)
    o_odd = jnp.zeros((16, FOX_W), F32)
    for i in range(0, nps, 2):
        pp = jnp.concatenate([p_parts[i], p_parts[i + 1]], axis=0).astype(BF16)
        o_even = o_even + _dot_nt(pp, v_refs[i][...].astype(BF16))
        o_odd = o_odd + _dot_nt(pp, v_refs[i + 1][...].astype(BF16))
    acc_s[...] = acc_s[...] * a + o_even[0:8, :] + o_odd[8:16, :]

    @pl.when(st == ns - 1)
    def _():
        lfn = _log_sigmoid(zgn_ref[...])
        lfo_ref[...] = lfn
        lfn_col = jnp.sum(jnp.where(eye8, jnp.broadcast_to(lfn, (8, 128)), 0.0), axis=1, keepdims=True)
        q8 = qbd_s[...].astype(F32)[0:8, :]
        s_n = jnp.sum(q8 * kn_ref[...], axis=1, keepdims=True) - (cc_s[...] + lfn_col)
        m = m_s[...]
        m_f = jnp.maximum(m, s_n)
        a = jnp.exp(m - m_f)
        p_n = jnp.exp(s_n - m_f)
        l_f = a * l_s[...] + p_n
        acc = (acc_s[...] * a + p_n * vn_ref[...]) / l_f
        o_ref[...] = jnp.sum(jnp.where(head8, acc, 0.0), axis=0, keepdims=True).astype(o_ref.dtype)


def fox_decode(page_table, q3, kn3, vn3, zgn3, pool_kT, pool_vT, pool_lfT, layer, n_pool, nps):
    DB, n_pages = page_table.shape
    base = layer * n_pool
    row = lambda n: pl.BlockSpec((None, 1, n), lambda b, s, pt: (b, 0, 0))

    def page_spec(i, r):
        return pl.BlockSpec((None, r, PAGE_SIZE), lambda b, s, pt: (base + pt[b, s * nps + i], 0, 0))

    in_specs = ([row(FOX_W), row(FOX_W), row(FOX_W), row(128)]
                + [page_spec(i, FOX_W) for i in range(nps)]
                + [page_spec(i, FOX_W) for i in range(nps)]
                + [page_spec(i, FOX_HEADS) for i in range(nps)])
    grid_spec = pltpu.PrefetchScalarGridSpec(
        num_scalar_prefetch=1,
        grid=(DB, n_pages // nps),
        in_specs=in_specs,
        out_specs=[row(FOX_W), row(128)],
        scratch_shapes=[pltpu.VMEM((16, FOX_W), BF16), pltpu.VMEM((8, 1), F32), pltpu.VMEM((8, 1), F32),
                        pltpu.VMEM((8, FOX_W), F32), pltpu.VMEM((8, 1), F32)])
    return pl.pallas_call(
        functools.partial(_fox_decode_kernel, nps),
        out_shape=[jax.ShapeDtypeStruct((DB, 1, FOX_W), BF16), jax.ShapeDtypeStruct((DB, 1, 128), F32)],
        grid_spec=grid_spec,
        compiler_params=_cp("parallel", "arbitrary"),
        name="fox_decode",
    )(page_table, q3, kn3, vn3, zgn3, *([pool_kT] * nps), *([pool_vT] * nps), *([pool_lfT] * nps))


def _outproj_kernel(yml_ref, ygla_ref, yfox_ref, h_ref, w_ref, g_ref, b_ref, wr_ref, br_ref, h1_ref, lg_ref):
    mix = (_dot(yml_ref[...], w_ref[0:ML_W, :]) + _dot(ygla_ref[...], w_ref[ML_W:ML_W + GLA_VW, :])
           + _dot(yfox_ref[...], w_ref[ML_W + GLA_VW:, :]))
    h1 = _ln_rows(DN_ALPHA * h_ref[...] + mix, g_ref[...], b_ref[...])
    h1_ref[...] = h1
    lg_ref[...] = _dot(h1.astype(BF16), wr_ref[...]) + br_ref[...]


def outproj(yml, ygla, yfox, h, w, g, b, wr, br, tm):
    T = h.shape[0]
    tok = lambda n: pl.BlockSpec((tm, n), lambda i: (i, 0))
    full = lambda r, n: pl.BlockSpec((r, n), lambda i: (0, 0))
    return pl.pallas_call(
        _outproj_kernel,
        out_shape=[jax.ShapeDtypeStruct((T, D_MODEL), F32), jax.ShapeDtypeStruct((T, 128), F32)],
        grid=(T // tm,),
        in_specs=[tok(ML_W), tok(GLA_VW), tok(FOX_W), tok(D_MODEL), full(D_MODEL, D_MODEL),
                  full(1, D_MODEL), full(1, D_MODEL), full(D_MODEL, 128), full(1, 128)],
        out_specs=[tok(D_MODEL), tok(128)],
        compiler_params=_cp("parallel"),
        name="outproj",
    )(yml, ygla, yfox, h, w, g, b, wr, br)


def _route_kernel(lg_ref, comb_ref):
    tm = lg_ref.shape[0]
    lt = lg_ref[...].T
    g = lt[_R_GRP:_R_GRP + N_GROUPS, :]
    e = lt[_R_EXP:_R_EXP + N_EXPERTS, :]
    gmax = jnp.max(g, axis=0, keepdims=True)
    gidx = _iota((N_GROUPS, tm), 0)
    g_sel = jnp.min(jnp.where(g == gmax, gidx, N_GROUPS), axis=0, keepdims=True)
    p_grp = 1.0 / jnp.sum(jnp.exp(g - gmax), axis=0, keepdims=True)
    eidx = _iota((N_EXPERTS, tm), 0)
    ev = jnp.where(eidx // EXPERTS_PER_GROUP == g_sel, e, NEG_INF)
    v1 = jnp.max(ev, axis=0, keepdims=True)
    i1 = jnp.min(jnp.where(ev == v1, eidx, N_EXPERTS), axis=0, keepdims=True)
    ev2 = jnp.where(eidx == i1, NEG_INF, ev)
    v2 = jnp.max(ev2, axis=0, keepdims=True)
    i2 = jnp.min(jnp.where(ev2 == v2, eidx, N_EXPERTS), axis=0, keepdims=True)
    t = jnp.exp(v2 - v1)
    w1 = p_grp / (1.0 + t)
    w2 = p_grp * t / (1.0 + t)
    comb = jnp.where(eidx == i1, w1, 0.0) + jnp.where(eidx == i2, w2, 0.0)
    comb128 = jnp.concatenate([comb, jnp.zeros((128 - N_EXPERTS, tm), F32)], axis=0)
    comb_ref[...] = comb128.T


def route(lg, tm):
    T = lg.shape[0]
    return pl.pallas_call(
        _route_kernel,
        out_shape=jax.ShapeDtypeStruct((T, 128), F32),
        grid=(T // tm,),
        in_specs=[pl.BlockSpec((tm, 128), lambda i: (i, 0))],
        out_specs=pl.BlockSpec((tm, 128), lambda i: (i, 0)),
        compiler_params=_cp("parallel"),
        name="route",
    )(lg)


def _moe_kernel(x_ref, comb_ref, wg_ref, wu_ref, wd_ref, g_ref, b_ref, o_ref, acc_s, xb_s):
    e = pl.program_id(1)

    @pl.when(e == 0)
    def _():
        acc_s[...] = jnp.zeros_like(acc_s)
        xb_s[...] = x_ref[...].astype(BF16)

    xb = xb_s[...]
    a = _dot(xb, wg_ref[...])
    u = _dot(xb, wu_ref[...])
    lane = _iota((1, 128), 1)
    ce = jnp.sum(jnp.where(lane == e, comb_ref[...], 0.0), axis=1, keepdims=True)
    hdn = (a * _sigmoid(a)) * u * ce
    acc_s[...] += _dot(hdn.astype(BF16), wd_ref[...])

    @pl.when(e == pl.num_programs(1) - 1)
    def _():
        o_ref[...] = _ln_rows(DN_ALPHA * x_ref[...] + acc_s[...], g_ref[...], b_ref[...])


def moe(x, comb, wg, wu, wd, g, b, tm):
    T = x.shape[0]
    return pl.pallas_call(
        _moe_kernel,
        out_shape=jax.ShapeDtypeStruct((T, D_MODEL), F32),
        grid=(T // tm, N_EXPERTS),
        in_specs=[pl.BlockSpec((tm, D_MODEL), lambda i, e: (i, 0)),
                  pl.BlockSpec((tm, 128), lambda i, e: (i, 0)),
                  pl.BlockSpec((None, D_MODEL, D_EXPERT), lambda i, e: (e, 0, 0)),
                  pl.BlockSpec((None, D_MODEL, D_EXPERT), lambda i, e: (e, 0, 0)),
                  pl.BlockSpec((None, D_EXPERT, D_MODEL), lambda i, e: (e, 0, 0)),
                  pl.BlockSpec((1, D_MODEL), lambda i, e: (0, 0)),
                  pl.BlockSpec((1, D_MODEL), lambda i, e: (0, 0))],
        out_specs=pl.BlockSpec((tm, D_MODEL), lambda i, e: (i, 0)),
        scratch_shapes=[pltpu.VMEM((tm, D_MODEL), F32), pltpu.VMEM((tm, D_MODEL), BF16)],
        compiler_params=_cp("parallel", "arbitrary"),
        name="moe",
    )(x, comb, wg, wu, wd, g, b)


def _perm_index():
    idx = []
    for name, n in (("ml_q", 256), ("ml_k", 256), ("ml_v", 256), ("ml_o", 256),
                    ("g_q", 128), ("g_k", 128), ("g_v", 256), ("g_r", 256),
                    ("fx_q", 512), ("fx_k", 512), ("fx_v", 512),
                    ("fx_f", 8), ("ml_i", 4), ("ml_f", 4), ("g_lr", 16)):
        idx.extend(range(_OFF[name], _OFF[name] + n))
    return np.asarray(idx, np.int32)


def _col_scale():
    s = np.ones((_NPAD,), np.float32)
    s[_ML0 + ML_W:_ML0 + 2 * ML_W] = ML_DH ** -0.5
    s[_FQ0:_FK0] = FOX_DH ** -0.5
    return s


def _prep_inproj(w_in, b_in):
    idx = _perm_index()
    pad = _NPAD - idx.shape[0]
    scale = jnp.asarray(_col_scale())
    w = jnp.pad(jnp.take(w_in, idx, axis=2), ((0, 0), (0, 0), (0, pad))) * scale
    b = jnp.pad(jnp.take(b_in, idx, axis=1), ((0, 0), (0, pad))) * scale
    return w.astype(BF16), b.reshape(DEPTH, 1, _NPAD)


def _prep_inproj_t(w_in, b_in):
    wt = jnp.transpose(w_in, (0, 2, 1))
    parts, bparts = [], []
    for name, sc in (("fx_q", FOX_DH ** -0.5), ("fx_k", 1.0), ("fx_v", 1.0)):
        parts.append(wt[:, _OFF[name]:_OFF[name] + FOX_W, :] * sc)
        bparts.append(b_in[:, _OFF[name]:_OFF[name] + FOX_W] * sc)
    return jnp.concatenate(parts, axis=1).astype(BF16), jnp.concatenate(bparts, axis=1)[:, :, None]


def _block_diag_tile(x, reps, blk_r, blk_c):
    B, R, _ = x.shape
    t = jnp.tile(x, (1, 1, reps))
    keep = (np.arange(R)[:, None] // blk_r) == (np.arange(reps * blk_c)[None, :] // blk_c)
    return jnp.where(jnp.asarray(keep), t, 0.0)


def _diag_blocks(x, heads, blk_r, blk_c):
    B = x.shape[0]
    x5 = x.reshape(B, heads, blk_r, heads, blk_c)
    return jnp.stack([x5[:, h, :, h, :] for h in range(heads)], axis=1)


def _pad_chunk(x):
    return jnp.pad(x[:, None, :], ((0, 0), (0, CHUNK - 1), (0, 0)))


def kernel(x_prompt, x_sample, cache_fox_k, cache_fox_v, cache_fox_logf, state_mlstm_C, state_mlstm_n, state_mlstm_m, state_gla_S, page_table, ln_in_g, ln_in_b, w_in, b_in, mlstm_norm_g, gla_w_gate_up, gla_b_gate, gla_norm_g, w_out, ln1_g, ln1_b, w_router_group, b_router_group, w_router_expert, b_router_expert, w_exp_gate, w_exp_up, w_exp_down, ln2_g, ln2_b):
    B, S, D = x_prompt.shape
    DB = x_sample.shape[0]
    T = B * S
    n_pool = cache_fox_k.shape[1]
    TS = 128

    w_in_p, b_in_p = _prep_inproj(w_in, b_in)
    w_out_b = w_out.astype(BF16)
    wr = jnp.zeros((DEPTH, D, 128), F32)
    wr = wr.at[:, :, _R_GRP:_R_GRP + N_GROUPS].set(w_router_group).at[:, :, _R_EXP:_R_EXP + N_EXPERTS].set(w_router_expert)
    wr = wr.astype(BF16)
    br = jnp.zeros((DEPTH, 1, 128), F32)
    br = br.at[:, 0, _R_GRP:_R_GRP + N_GROUPS].set(b_router_group).at[:, 0, _R_EXP:_R_EXP + N_EXPERTS].set(b_router_expert)
    wg_b = w_exp_gate.astype(BF16)
    wu_b = w_exp_up.astype(BF16)
    wd_b = w_exp_down.astype(BF16)
    wgl = jnp.zeros((DEPTH, 128, GLA_KW), F32).at[:, _G_LR:_G_LR + GLA_RANK, :].set(gla_w_gate_up).astype(BF16)
    row = lambda a: a.reshape(DEPTH, 1, -1)
    ln1g, ln1b, ln2g, ln2b = row(ln1_g), row(ln1_b), row(ln2_g), row(ln2_b)
    mlng, glng, glbg = row(mlstm_norm_g), row(gla_norm_g), row(gla_b_gate)
    w_in_t, b_in_t = _prep_inproj_t(w_in, b_in)
    pool_k = jnp.transpose(cache_fox_k, (0, 1, 3, 4, 2)).reshape(DEPTH * n_pool, FOX_W, PAGE_SIZE)
    pool_v = jnp.transpose(cache_fox_v, (0, 1, 3, 4, 2)).reshape(DEPTH * n_pool, FOX_W, PAGE_SIZE)
    pool_lf = jnp.transpose(cache_fox_logf, (0, 1, 3, 2)).reshape(DEPTH * n_pool, FOX_HEADS, PAGE_SIZE)

    g_in, b_in_ln = ln_in_g.reshape(1, D), ln_in_b.reshape(1, D)
    hp = ln_rows(x_prompt.reshape(T, D), g_in, b_in_ln, 512)
    xs = jnp.pad(x_sample.reshape(DB, D), ((0, TS - DB), (0, 0)))
    hs = ln_rows(xs, g_in, b_in_ln, TS)

    zc = jnp.zeros((B, ML_W, ML_W), F32)
    zn = jnp.zeros((B, 1, ML_W), F32)
    zm = jnp.zeros((B, 1, 128), F32)
    zs = jnp.zeros((B, GLA_KW, GLA_VW), F32)

    outs = {k: [] for k in ("lfp", "ks", "vs", "lfs", "cp", "np", "mp", "cs", "ns", "ms", "gp", "gs")}
    kv_all = None
    tq = min(FOX_TQ, S)
    for l in range(DEPTH):
        zml, zgla, zg, k_rm, qT, kT_all, vT_all = inproj_t(hp, w_in_p[l], b_in_p[l], w_in_t[l], b_in_t[l],
                                                           kv_all, l, B, S, tq)
        kv_all = (kT_all, vT_all)
        zg3 = zg.reshape(B, S, 128)
        lf, c_sh = foxgate(zg3)
        y_fox = fox_attn(qT, k_rm, vT_all, c_sh, l, B, S, tq)
        y_ml, c_o, n_o, m_o = mlstm(zml.reshape(B, S, 4 * ML_W), zg3, zc, zn, zm, mlng[l], CHUNK, min(MLSTM_BB, B))
        y_gla, s_o = gla(zgla.reshape(B, S, -1), zg3, wgl[l], glbg[l], zs, glng[l], CHUNK, min(GLA_BB, B))
        h1, lg = outproj(y_ml.reshape(T, ML_W), y_gla.reshape(T, GLA_VW), y_fox, hp, w_out_b[l],
                         ln1g[l], ln1b[l], wr[l], br[l], 512)
        comb = route(lg, 512)
        hp = moe(h1, comb, wg_b[l], wu_b[l], wd_b[l], ln2g[l], ln2b[l], min(1024, T))
        outs["lfp"].append(lf)
        outs["cp"].append(_diag_blocks(c_o, ML_HEADS, ML_DH, ML_DH))
        outs["np"].append(n_o.reshape(B, ML_HEADS, ML_DH)); outs["mp"].append(m_o[:, 0, :ML_HEADS])
        outs["gp"].append(_diag_blocks(s_o, GLA_HEADS, GLA_DK, GLA_DV))

        zml, zgla, fq, fk, fv, zg = inproj(hs, w_in_p[l], b_in_p[l], TS)
        zg3 = _pad_chunk(zg[:DB])
        c0 = _block_diag_tile(state_mlstm_C[l].astype(F32).reshape(DB, ML_W, ML_DH), ML_HEADS, ML_DH, ML_DH)
        n0 = state_mlstm_n[l].astype(F32).reshape(DB, 1, ML_W)
        m0 = jnp.pad(state_mlstm_m[l].astype(F32), ((0, 0), (0, 128 - ML_HEADS))).reshape(DB, 1, 128)
        s0 = _block_diag_tile(state_gla_S[l].astype(F32).reshape(DB, GLA_KW, GLA_DV), GLA_HEADS, GLA_DK, GLA_DV)
        y_ml, c_o, n_o, m_o = mlstm(_pad_chunk(zml[:DB]), zg3, c0, n0, m0, mlng[l], 1, 4)
        y_gla, s_o = gla(_pad_chunk(zgla[:DB]), zg3, wgl[l], glbg[l], s0, glng[l], 1, 4)
        y_fox, lfn = fox_decode(page_table, fq[:DB].astype(F32).reshape(DB, 1, FOX_W),
                                fk[:DB].reshape(DB, 1, FOX_W), fv[:DB].reshape(DB, 1, FOX_W),
                                zg[:DB].reshape(DB, 1, 128), pool_k, pool_v, pool_lf, l, n_pool,
                                min(16, page_table.shape[1]))
        padr = lambda a: jnp.pad(a, ((0, TS - DB), (0, 0)))
        h1, lg = outproj(padr(y_ml[:, 0, :]), padr(y_gla[:, 0, :]), padr(y_fox.reshape(DB, FOX_W)), hs, w_out_b[l],
                         ln1g[l], ln1b[l], wr[l], br[l], TS)
        comb = route(lg, TS)
        hs = moe(h1, comb, wg_b[l], wu_b[l], wd_b[l], ln2g[l], ln2b[l], TS)
        outs["ks"].append(fk[:DB]); outs["vs"].append(fv[:DB]); outs["lfs"].append(lfn[:, 0, :FOX_HEADS])
        outs["cs"].append(_diag_blocks(c_o, ML_HEADS, ML_DH, ML_DH))
        outs["ns"].append(n_o.reshape(DB, ML_HEADS, ML_DH)); outs["ms"].append(m_o[:, 0, :ML_HEADS])
        outs["gs"].append(_diag_blocks(s_o, GLA_HEADS, GLA_DK, GLA_DV))

    st = lambda k, shape, dt: jnp.stack(outs[k]).reshape((DEPTH,) + shape).astype(dt)
    kd, vd, ld = cache_fox_k.dtype, cache_fox_v.dtype, cache_fox_logf.dtype
    cd, nd, md, sd = state_mlstm_C.dtype, state_mlstm_n.dtype, state_mlstm_m.dtype, state_gla_S.dtype
    kvp = lambda a, dt: jnp.transpose(a.reshape(DEPTH, B, FOX_HEADS, FOX_DH, S), (0, 1, 4, 2, 3)).astype(dt)
    lfp = jnp.transpose(jnp.stack(outs["lfp"]), (0, 1, 3, 2)).astype(ld)
    return (hp.reshape(B, S, D), hs[:DB].reshape(DB, 1, D),
            kvp(kv_all[0], kd), kvp(kv_all[1], vd), lfp,
            st("ks", (DB, 1, FOX_HEADS, FOX_DH), kd), st("vs", (DB, 1, FOX_HEADS, FOX_DH), vd),
            st("lfs", (DB, 1, FOX_HEADS), ld),
            st("cp", (B, ML_HEADS, ML_DH, ML_DH), cd), st("np", (B, ML_HEADS, ML_DH), nd), st("mp", (B, ML_HEADS), md),
            st("cs", (DB, ML_HEADS, ML_DH, ML_DH), cd), st("ns", (DB, ML_HEADS, ML_DH), nd), st("ms", (DB, ML_HEADS), md),
            st("gp", (B, GLA_HEADS, GLA_DK, GLA_DV), sd), st("gs", (DB, GLA_HEADS, GLA_DK, GLA_DV), sd))
```

```python
import functools

import numpy as np
import jax
import jax.numpy as jnp
from jax import lax
from jax.experimental import pallas as pl
from jax.experimental.pallas import tpu as pltpu

F32 = jnp.float32
BF16 = jnp.bfloat16
HI = lax.Precision.HIGHEST

D_MODEL = 1024
DEPTH = 4
PAGE_SIZE = 128
ML_DH = 64
ML_HEADS = 4
ML_W = 256
GLA_DK = 32
GLA_DV = 64
GLA_HEADS = 4
GLA_KW = 128
GLA_VW = 256
GLA_RANK = 16
GLA_TAU = 16.0
FOX_DH = 64
FOX_HEADS = 8
FOX_W = 512
CHUNK = 128
SUB = 16
N_GROUPS = 4
EXPERTS_PER_GROUP = 4
N_EXPERTS = 16
D_EXPERT = 256
DN_ALPHA = (2.0 * DEPTH) ** 0.25
LN_EPS = 1e-5
NORM_EPS = 1e-6
NEG_INF = float("-inf")
LOG2E = 1.4426950408889634

_OFF = dict(ml_q=0, ml_k=256, ml_v=512, ml_o=768, ml_i=1024, ml_f=1028, g_q=1032, g_k=1160, g_v=1288,
            g_r=1544, g_lr=1800, fx_q=1816, fx_k=2328, fx_v=2840, fx_f=3352)
_ML0, _GLA0, _FQ0, _FK0, _FV0, _ZG0, _NPAD = 0, 1024, 1792, 2304, 2816, 3328, 3456
_G_FXF, _G_MLI, _G_MLF, _G_LR = 0, 8, 12, 16
_R_GRP, _R_EXP = 0, 8

VMEM_LIMIT = 56 * 1024 * 1024
FOX_TQ = 1024
INPROJ_TM = 512
SAMPLE_BB = 8
FOX_KS, FOX_QS = 128, 256
FOX_LOOKAHEAD = 8
MLSTM_BB, GLA_BB = 4, 4


def _cp(*sem):
    return pltpu.CompilerParams(dimension_semantics=sem, vmem_limit_bytes=VMEM_LIMIT)


def _dot(a, b, precision=None):
    return jnp.dot(a, b, preferred_element_type=F32, precision=precision)


def _dot_nt(a, b):
    return lax.dot_general(a, b, (((1,), (1,)), ((), ())), preferred_element_type=F32)


def _dot_tn(a, b):
    return lax.dot_general(a, b, (((0,), (0,)), ((), ())), preferred_element_type=F32)


def _split_bf16(x, terms):
    parts, r = [], x
    for _ in range(terms):
        p = r.astype(BF16)
        parts.append(p)
        r = r - p.astype(F32)
    return parts


def _dot_exact_l(sel, x, terms):
    out = None
    for p in _split_bf16(x, terms):
        d = _dot(sel, p)
        out = d if out is None else out + d
    return out


def _dot_exact_r(x, sel, terms):
    out = None
    for p in _split_bf16(x, terms):
        d = _dot(p, sel)
        out = d if out is None else out + d
    return out


def _log_sigmoid(x):
    return jnp.minimum(x, 0.0) - jnp.log1p(jnp.exp(-jnp.abs(x)))


def _sigmoid(x):
    return 1.0 / (1.0 + jnp.exp(-x))


def _ln_rows(x, g, b):
    mu = jnp.mean(x, axis=-1, keepdims=True)
    xc = x - mu
    var = jnp.mean(xc * xc, axis=-1, keepdims=True)
    return xc * lax.rsqrt(var + LN_EPS) * g + b


def _iota(shape, dim):
    return lax.broadcasted_iota(jnp.int32, shape, dim)


def _ln_kernel(x_ref, g_ref, b_ref, o_ref):
    o_ref[...] = _ln_rows(x_ref[...], g_ref[...], b_ref[...])


def ln_rows(x, g, b, tm):
    T, D = x.shape
    return pl.pallas_call(
        _ln_kernel,
        out_shape=jax.ShapeDtypeStruct((T, D), F32),
        grid=(T // tm,),
        in_specs=[pl.BlockSpec((tm, D), lambda i: (i, 0)),
                  pl.BlockSpec((1, D), lambda i: (0, 0)),
                  pl.BlockSpec((1, D), lambda i: (0, 0))],
        out_specs=pl.BlockSpec((tm, D), lambda i: (i, 0)),
        compiler_params=_cp("parallel"),
        name="ln_in",
    )(x, g, b)


_IN_BOUNDS = ((_ML0, _GLA0), (_GLA0, _FQ0), (_FQ0, _FK0), (_FK0, _FV0), (_FV0, _ZG0), (_ZG0, _NPAD))


def _inproj_kernel(h_ref, w_ref, b_ref, *out_refs):
    xb = h_ref[...].astype(BF16)
    for ref, (a, b) in zip(out_refs, _IN_BOUNDS):
        ref[...] = (_dot(xb, w_ref[:, a:b]) + b_ref[:, a:b]).astype(ref.dtype)


def inproj(h, w, b, tm):
    T = h.shape[0]
    widths = [b_ - a_ for a_, b_ in _IN_BOUNDS]
    dtypes = [F32, F32, BF16, F32, F32, F32]
    return pl.pallas_call(
        _inproj_kernel,
        out_shape=[jax.ShapeDtypeStruct((T, n), dt) for n, dt in zip(widths, dtypes)],
        grid=(T // tm,),
        in_specs=[pl.BlockSpec((tm, D_MODEL), lambda i: (i, 0)),
                  pl.BlockSpec((D_MODEL, _NPAD), lambda i: (0, 0)),
                  pl.BlockSpec((1, _NPAD), lambda i: (0, 0))],
        out_specs=[pl.BlockSpec((tm, n), lambda i: (i, 0)) for n in widths],
        compiler_params=_cp("parallel"),
        name="inproj",
    )(h, w, b)


def _inproj_t_kernel(h_ref, w_ref, b_ref, wt_ref, bt_ref, *refs):
    ml_ref, gla_ref, zg_ref, krm_ref, qT_ref, kT_ref, vT_ref = refs[-7:]
    xb = h_ref[...].astype(BF16)
    for ref, (a, b) in ((ml_ref, (_ML0, _GLA0)), (gla_ref, (_GLA0, _FQ0)), (zg_ref, (_ZG0, _NPAD)),
                        (krm_ref, (_FK0, _FV0))):
        ref[...] = (_dot(xb, w_ref[:, a:b]) + b_ref[:, a:b]).astype(ref.dtype)
    for t, ref in enumerate((qT_ref, kT_ref, vT_ref)):
        sl = slice(t * FOX_W, (t + 1) * FOX_W)
        ref[...] = (_dot_nt(wt_ref[sl, :], xb) + bt_ref[sl, :]).astype(ref.dtype)


def inproj_t(h, w, b, wt, bt, kv_all, layer, B, S, tm):
    T = h.shape[0]
    nb = S // tm
    tok = lambda n: pl.BlockSpec((tm, n), lambda i: (i, 0))
    full = lambda r, n: pl.BlockSpec((r, n), lambda i: (0, 0))
    kv_spec = pl.BlockSpec((None, None, FOX_W, tm), lambda i: (layer, i // nb, 0, i % nb))
    kv_shape = jax.ShapeDtypeStruct((DEPTH, B, FOX_W, S), F32)
    in_specs = [tok(D_MODEL), full(D_MODEL, _NPAD), full(1, _NPAD), full(3 * FOX_W, D_MODEL), full(3 * FOX_W, 1)]
    args = [h, w, b, wt, bt]
    aliases = {}
    if kv_all is not None:
        in_specs += [pl.BlockSpec(memory_space=pl.ANY), pl.BlockSpec(memory_space=pl.ANY)]
        args += list(kv_all)
        aliases = {5: 5, 6: 6}
    return pl.pallas_call(
        _inproj_t_kernel,
        out_shape=[jax.ShapeDtypeStruct((T, 4 * ML_W), F32), jax.ShapeDtypeStruct((T, _FQ0 - _GLA0), F32),
                   jax.ShapeDtypeStruct((T, 128), F32), jax.ShapeDtypeStruct((T, FOX_W), BF16),
                   jax.ShapeDtypeStruct((B, FOX_W, S), BF16), kv_shape, kv_shape],
        grid=(T // tm,),
        in_specs=in_specs,
        out_specs=[tok(4 * ML_W), tok(_FQ0 - _GLA0), tok(128), tok(FOX_W),
                   pl.BlockSpec((None, FOX_W, tm), lambda i: (i // nb, 0, i % nb)), kv_spec, kv_spec],
        input_output_aliases=aliases,
        compiler_params=_cp("arbitrary"),
        name="inproj_t",
    )(*args)


def _foxgate_kernel(zg_ref, lft_ref, c_ref):
    S = zg_ref.shape[0]
    tri = (_iota((CHUNK, CHUNK), 0) >= _iota((CHUNK, CHUNK), 1)).astype(BF16)
    src = _iota((128, FOX_W), 0)
    dst = _iota((128, FOX_W), 1)
    place = [jnp.where((dst == 128 * (src // 2) + 3 * (src % 2) + t) & (src < FOX_HEADS), 1.0, 0.0).astype(BF16)
             for t in range(3)]
    carry = jnp.zeros((1, 128), F32)
    for j in range(S // CHUNK):
        sl = slice(j * CHUNK, (j + 1) * CHUNK)
        lf = _log_sigmoid(zg_ref[sl, :])
        lft_ref[:, sl] = lf.T[_G_FXF:_G_FXF + FOX_HEADS, :]
        cs = _dot_exact_l(tri, lf, 3) + carry
        carry = cs[CHUNK - 1:CHUNK, :]
        out = None
        for part, pl_t in zip(_split_bf16(cs * LOG2E, 3), place):
            d = _dot(part, pl_t)
            out = d if out is None else out + d
        c_ref[sl, :] = out.astype(BF16)


def foxgate(zg3):
    B, S, _ = zg3.shape
    return pl.pallas_call(
        _foxgate_kernel,
        out_shape=[jax.ShapeDtypeStruct((B, FOX_HEADS, S), F32), jax.ShapeDtypeStruct((B * S, FOX_W), BF16)],
        grid=(B,),
        in_specs=[pl.BlockSpec((None, S, 128), lambda b: (b, 0, 0))],
        out_specs=[pl.BlockSpec((None, FOX_HEADS, S), lambda b: (b, 0, 0)),
                   pl.BlockSpec((S, FOX_W), lambda b: (b, 0))],
        compiler_params=_cp("parallel"),
        name="foxgate",
    )(zg3)


def _fox_attn_kernel(tq, qT_ref, k_ref, vT_ref, c_ref, o_ref, vb_s):
    qi = pl.program_id(2)

    @pl.when(qi == 0)
    def _():
        vb_s[...] = vT_ref[...].astype(BF16)

    qT = qT_ref[...]
    rowp = _iota((128, 1), 0)
    zero = jnp.zeros_like(qT)
    q_head = (jnp.where(rowp < FOX_DH, qT, zero), jnp.where(rowp >= FOX_DH, qT, zero))
    qTh = tuple(jnp.concatenate(
        [q_head[hh], jnp.broadcast_to(jnp.where(rowp // 3 == hh, -1.0, 0.0), qT.shape).astype(BF16)], axis=0)
        for hh in range(2))
    ks, qs = min(FOX_KS, tq), min(FOX_QS, tq)
    nqs = tq // qs
    tri_mask = _iota((ks, qs), 0) - _iota((ks, qs), 1)

    def block(j, carry, masked):
        off = pl.multiple_of(j * tq, tq)
        carry = list(carry)
        tiles = [(kk, hh, t) for kk in range(tq // ks) for hh in range(2) for t in range(nqs)
                 if not (masked and kk * ks >= (t + 1) * qs)]

        def scores(kk, hh, t):
            ko = pl.multiple_of(off + kk * ks, ks)
            kc = jnp.concatenate([k_ref[pl.ds(ko, ks), :], c_ref[pl.ds(ko, ks), :]], axis=1)
            s = _dot(kc, qTh[hh][:, t * qs:(t + 1) * qs])
            if masked and (kk + 1) * ks - 1 > t * qs:
                s = jnp.where(tri_mask <= t * qs - kk * ks, s, NEG_INF)
            return s

        ready = [scores(*tl) for tl in tiles[:FOX_LOOKAHEAD]]
        for i, (kk, hh, t) in enumerate(tiles):
            if i + FOX_LOOKAHEAD < len(tiles):
                ready.append(scores(*tiles[i + FOX_LOOKAHEAD]))
            s = ready[i]
            ko = pl.multiple_of(off + kk * ks, ks)
            m, l, acc = carry[hh * nqs + t]
            m_new = jnp.maximum(m, jnp.max(s, axis=0, keepdims=True))
            a = jnp.exp2(m - m_new)
            p = jnp.exp2(s - m_new)
            l = a * l + jnp.sum(p, axis=0, keepdims=True)
            vb = vb_s[hh * FOX_DH:(hh + 1) * FOX_DH, pl.ds(ko, ks)]
            acc = a * acc + _dot(vb, p.astype(BF16))
            carry[hh * nqs + t] = (m_new, l, acc)
        return tuple(carry)

    init = tuple((jnp.full((1, qs), NEG_INF, F32), jnp.zeros((1, qs), F32), jnp.zeros((FOX_DH, qs), F32))
                 for _ in range(2 * nqs))
    carry = lax.fori_loop(0, qi, functools.partial(block, masked=False), init)
    carry = block(qi, carry, True)
    oT = jnp.concatenate(
        [jnp.concatenate([carry[hh * nqs + t][2] / carry[hh * nqs + t][1] for t in range(nqs)], axis=1)
         for hh in range(2)], axis=0)
    o_ref[...] = oT.T.astype(o_ref.dtype)


def fox_attn(qT, k_rm, vT_all, c_sh, layer, B, S, tq):
    T = B * S
    nq = S // tq
    return pl.pallas_call(
        functools.partial(_fox_attn_kernel, tq),
        out_shape=jax.ShapeDtypeStruct((T, FOX_W), BF16),
        grid=(B, FOX_HEADS // 2, nq),
        in_specs=[pl.BlockSpec((None, 128, tq), lambda b, hp, qi: (b, hp, qi)),
                  pl.BlockSpec((S, 128), lambda b, hp, qi: (b, hp)),
                  pl.BlockSpec((None, None, 128, S), lambda b, hp, qi: (layer, b, hp, 0)),
                  pl.BlockSpec((S, 128), lambda b, hp, qi: (b, hp))],
        out_specs=pl.BlockSpec((tq, 128), lambda b, hp, qi: (b * nq + qi, hp)),
        scratch_shapes=[pltpu.VMEM((128, S), BF16)],
        compiler_params=_cp("parallel", "arbitrary", "arbitrary"),
        name="fox_attn",
    )(qT, k_rm, vT_all, c_sh)


def _mlstm_kernel(n_valid, bb, z_ref, zg_ref, c0_ref, n0_ref, m0_ref, ng_ref,
                  y_ref, co_ref, no_ref, mo_ref, c_s, n_s, m_s):
    c = pl.program_id(1)
    nc = pl.num_programs(1)
    L = CHUNK

    @pl.when(c == 0)
    def _():
        c_s[...] = c0_ref[...]
        n_s[...] = n0_ref[...]
        m_s[...] = m0_ref[...]

    row = _iota((L, L), 0)
    col = _iota((L, L), 1)
    causal = row >= col
    tri = causal.astype(BF16)
    lane256 = _iota((1, ML_W), 1)
    lane128 = _iota((1, 128), 1)
    valid = _iota((L, 1), 0) < n_valid
    bd = _iota((ML_W, ML_W), 0) // ML_DH == _iota((ML_W, ML_W), 1) // ML_DH
    seg = (_iota((ML_W, 128), 0) // ML_DH == _iota((ML_W, 128), 1)).astype(BF16)
    segmean = (bd.astype(F32) * (1.0 / ML_DH)).astype(BF16)
    ng = ng_ref[...]

    elems = range(bb)
    units = [(i, h) for i in elems for h in range(ML_HEADS)]
    hms = [lane256 // ML_DH == h for h in range(ML_HEADS)]

    E = []
    for i in elems:
        z = z_ref[i]
        q = z[:, 0:ML_W]
        k = z[:, ML_W:2 * ML_W]
        v = z[:, 2 * ML_W:3 * ML_W]
        g = zg_ref[i]
        li_all = g
        lf_all = _log_sigmoid(g)
        if n_valid < L:
            li_all = jnp.where(valid, li_all, NEG_INF)
            lf_all = jnp.where(valid, lf_all, 0.0)
        E.append(dict(q=q, k=k, li_all=li_all, lf_all=lf_all, kb=k.astype(BF16), vb=v.astype(BF16),
                      C=c_s[i], n_row=n_s[i], m_row=m_s[i]))
    for e in E:
        e["b_all"] = _dot_exact_l(tri, e["lf_all"], 3)
    for e in E:
        e["qC"] = _dot(e["q"].astype(BF16), e["C"].astype(BF16))
    for e in E:
        e["qn"] = _dot_exact_r(e["q"] * e["n_row"], seg, 2)
    for e in E:
        e["liT"] = e["li_all"].T
        e["bT"] = e["b_all"].T

    U = {}
    for (i, h) in units:
        e = E[i]
        U[i, h] = dict(qk=_dot_nt(jnp.where(hms[h], e["q"], 0.0).astype(BF16), e["kb"]))

    for (i, h) in units:
        e, u = E[i], U[i, h]
        b_col = e["b_all"][:, _G_MLF + h:_G_MLF + h + 1]
        b_row = e["bT"][_G_MLF + h:_G_MLF + h + 1, :]
        li_row = e["liT"][_G_MLI + h:_G_MLI + h + 1, :]
        m_prev = e["m_row"][:, h:h + 1]
        dmat = jnp.where(causal, b_col - b_row + li_row, NEG_INF)
        inter = b_col + m_prev
        m_t = jnp.maximum(inter, jnp.max(dmat, axis=1, keepdims=True))
        u.update(b_col=b_col, m_prev=m_prev, m_t=m_t, w_prev=jnp.exp(inter - m_t))
        u["s"] = u["qk"] * jnp.exp(dmat - m_t)

    for (i, h) in units:
        U[i, h]["num"] = _dot(U[i, h]["s"].astype(BF16), E[i]["vb"])

    for e in E:
        e.update(h_acc=jnp.zeros((L, ML_W), F32), kg=jnp.zeros((L, ML_W), F32),
                 gp_row=jnp.zeros((1, ML_W), F32), m_new_row=e["m_row"])
    for (i, h) in units:
        e, u = E[i], U[i, h]
        m_t, w_prev, b_col = u["m_t"], u["w_prev"], u["b_col"]
        den = jnp.sum(u["s"], axis=1, keepdims=True) + w_prev * e["qn"][:, h:h + 1]
        den = jnp.maximum(jnp.abs(den), jnp.exp(-m_t))
        e["h_acc"] = jnp.where(hms[h], (u["num"] + e["qC"] * w_prev) / den, e["h_acc"])
        b_last = b_col[L - 1:L, :]
        m_new = m_t[L - 1:L, :]
        li_col = e["li_all"][:, _G_MLI + h:_G_MLI + h + 1]
        g_rows = jnp.exp(b_last - b_col + li_col - m_new)
        e["kg"] = jnp.where(hms[h], e["k"] * g_rows, e["kg"])
        e["gp_row"] = jnp.where(hms[h], jnp.exp(b_last + u["m_prev"] - m_new), e["gp_row"])
        e["m_new_row"] = jnp.where(lane128 == h, m_new, e["m_new_row"])

    for e in E:
        e["upd"] = _dot_tn(e["kg"].astype(BF16), e["vb"])
    for e in E:
        e["ms"] = _dot_exact_r(e["h_acc"] * e["h_acc"], segmean, 2)
    for i, e in zip(elems, E):
        c_s[i] = e["C"] * e["gp_row"] + jnp.where(bd, e["upd"], 0.0)
        n_s[i] = e["n_row"] * e["gp_row"] + jnp.sum(e["kg"], axis=0, keepdims=True)
        m_s[i] = e["m_new_row"]
        og = z_ref[i][:, 3 * ML_W:4 * ML_W]
        y = e["h_acc"] * lax.rsqrt(e["ms"] + NORM_EPS) * ng * _sigmoid(og)
        y_ref[i] = y.astype(y_ref.dtype)

    @pl.when(c == nc - 1)
    def _():
        co_ref[...] = c_s[...]
        no_ref[...] = n_s[...]
        mo_ref[...] = m_s[...]


def mlstm(z3, zg3, c0, n0, m0, ng, n_valid, bb):
    B, S, _ = z3.shape
    nc = S // CHUNK
    st = lambda shape: pl.BlockSpec((bb,) + shape, lambda b, c: (b, 0, 0))
    return pl.pallas_call(
        functools.partial(_mlstm_kernel, n_valid, bb),
        out_shape=[jax.ShapeDtypeStruct((B, S, ML_W), BF16),
                   jax.ShapeDtypeStruct((B, ML_W, ML_W), F32),
                   jax.ShapeDtypeStruct((B, 1, ML_W), F32),
                   jax.ShapeDtypeStruct((B, 1, 128), F32)],
        grid=(B // bb, nc),
        in_specs=[pl.BlockSpec((bb, CHUNK, 4 * ML_W), lambda b, c: (b, c, 0)),
                  pl.BlockSpec((bb, CHUNK, 128), lambda b, c: (b, c, 0)),
                  st((ML_W, ML_W)), st((1, ML_W)), st((1, 128)),
                  pl.BlockSpec((1, ML_W), lambda b, c: (0, 0))],
        out_specs=[pl.BlockSpec((bb, CHUNK, ML_W), lambda b, c: (b, c, 0)),
                   st((ML_W, ML_W)), st((1, ML_W)), st((1, 128))],
        scratch_shapes=[pltpu.VMEM((bb, ML_W, ML_W), F32), pltpu.VMEM((bb, 1, ML_W), F32),
                        pltpu.VMEM((bb, 1, 128), F32)],
        compiler_params=_cp("parallel", "arbitrary"),
        name="mlstm",
    )(z3, zg3, c0, n0, m0, ng)


def _gla_kernel(n_valid, bb, z_ref, zg_ref, wg_ref, bg_ref, s0_ref, ng_ref, y_ref, so_ref, s_s, o_s):
    c = pl.program_id(1)
    nc = pl.num_programs(1)
    L = CHUNK

    @pl.when(c == 0)
    def _():
        s_s[...] = s0_ref[...]

    tri = (_iota((L, L), 0) >= _iota((L, L), 1)).astype(BF16)
    lane128 = _iota((1, GLA_KW), 1)
    lane256 = _iota((1, GLA_VW), 1)
    rowi = _iota((L, 1), 0)
    valid = rowi < n_valid
    bd = _iota((GLA_KW, GLA_VW), 0) // GLA_DK == _iota((GLA_KW, GLA_VW), 1) // GLA_DV
    segexp = bd.astype(BF16)
    bdv = _iota((GLA_VW, GLA_VW), 0) // GLA_DV == _iota((GLA_VW, GLA_VW), 1) // GLA_DV
    segmean = (bdv.astype(F32) * (1.0 / GLA_DV)).astype(BF16)
    ng = ng_ref[...]
    hm128 = [lane128 // GLA_DK == h for h in range(GLA_HEADS)]
    hm256 = [lane256 // GLA_DV == h for h in range(GLA_HEADS)]

    E = []
    for i in range(bb):
        z = z_ref[i]
        E.append(dict(q=z[:, 0:GLA_KW] * (GLA_DK ** -0.5), k=z[:, GLA_KW:2 * GLA_KW],
                      v=z[:, 2 * GLA_KW:2 * GLA_KW + GLA_VW], S=s_s[i]))
    for i, e in enumerate(E):
        e["zz"] = _dot(zg_ref[i].astype(BF16), wg_ref[...]) + bg_ref[...]
    for e in E:
        loga = _log_sigmoid(e["zz"]) * (1.0 / GLA_TAU)
        if n_valid < L:
            loga = jnp.where(valid, loga, 0.0)
            e["k"] = jnp.where(valid, e["k"], 0.0)
        e["bc"] = _dot_exact_l(tri, loga, 3)
        e["vb"] = e["v"].astype(BF16)
    for i, e in enumerate(E):
        o_s[i] = _dot((e["q"] * jnp.exp(e["bc"])).astype(BF16), e["S"].astype(BF16))
    for j in range(L // SUB - 1):
        r0 = SUB * (j + 1)
        for e in E:
            q, k, v, bc = e["q"], e["k"], e["v"], e["bc"]
            e_j = bc[r0 - 1:r0, :]
            kt = k[r0 - SUB:r0, :] * jnp.exp(e_j - bc[r0 - SUB:r0, :])
            qt = q[r0:, :] * jnp.exp(bc[r0:, :] - e_j)
            kst = jnp.concatenate([jnp.where(hm, kt, 0.0) for hm in hm128], axis=0)
            e["a"] = _dot_nt(qt.astype(BF16), kst.astype(BF16))
        for i, e in enumerate(E):
            vj = e["v"][r0 - SUB:r0, :]
            vst = jnp.concatenate([jnp.where(hm, vj, 0.0) for hm in hm256], axis=0)
            o_s[i, r0:, :] += _dot(e["a"].astype(BF16), vst.astype(BF16))
    for e in E:
        e["o_diag"] = jnp.zeros((L, GLA_VW), F32)
    for d in range(SUB):
        ok = (rowi % SUB) + d < SUB
        for e in E:
            q, k, bc = e["q"], e["k"], e["bc"]
            qd = q if d == 0 else pltpu.roll(q, L - d, 0)
            bcd = bc if d == 0 else pltpu.roll(bc, L - d, 0)
            p = jnp.where(ok, qd * k * jnp.exp(jnp.where(ok, bcd - bc, 0.0)), 0.0)
            e["u"] = _dot(p.astype(BF16), segexp)
        for e in E:
            u = e["u"] * e["v"]
            e["o_diag"] = e["o_diag"] + (u if d == 0 else pltpu.roll(u, d, 0))
    for e in E:
        bc = e["bc"]
        last = bc[L - 1:L, :]
        e["upd"] = _dot_tn((e["k"] * jnp.exp(last - bc)).astype(BF16), e["vb"])
    for i, e in enumerate(E):
        e["o"] = o_s[i] + e["o_diag"]
        e["ms"] = _dot_exact_r(e["o"] * e["o"], segmean, 2)
    for i, e in enumerate(E):
        last_col = e["bc"].T[:, L - 1:L]
        s_s[i] = e["S"] * jnp.exp(last_col) + jnp.where(bd, e["upd"], 0.0)
        r = z_ref[i][:, 2 * GLA_KW + GLA_VW:2 * GLA_KW + 2 * GLA_VW]
        y = e["o"] * lax.rsqrt(e["ms"] + NORM_EPS) * ng * (r * _sigmoid(r))
        y_ref[i] = y.astype(y_ref.dtype)

    @pl.when(c == nc - 1)
    def _():
        so_ref[...] = s_s[...]


def gla(z3, zg3, wg, bg, s0, ng, n_valid, bb):
    B, S, _ = z3.shape
    nc = S // CHUNK
    return pl.pallas_call(
        functools.partial(_gla_kernel, n_valid, bb),
        out_shape=[jax.ShapeDtypeStruct((B, S, GLA_VW), BF16),
                   jax.ShapeDtypeStruct((B, GLA_KW, GLA_VW), F32)],
        grid=(B // bb, nc),
        in_specs=[pl.BlockSpec((bb, CHUNK, 2 * GLA_KW + 2 * GLA_VW), lambda b, c: (b, c, 0)),
                  pl.BlockSpec((bb, CHUNK, 128), lambda b, c: (b, c, 0)),
                  pl.BlockSpec((128, GLA_KW), lambda b, c: (0, 0)),
                  pl.BlockSpec((1, GLA_KW), lambda b, c: (0, 0)),
                  pl.BlockSpec((bb, GLA_KW, GLA_VW), lambda b, c: (b, 0, 0)),
                  pl.BlockSpec((1, GLA_VW), lambda b, c: (0, 0))],
        out_specs=[pl.BlockSpec((bb, CHUNK, GLA_VW), lambda b, c: (b, c, 0)),
                   pl.BlockSpec((bb, GLA_KW, GLA_VW), lambda b, c: (b, 0, 0))],
        scratch_shapes=[pltpu.VMEM((bb, GLA_KW, GLA_VW), F32), pltpu.VMEM((bb, CHUNK, GLA_VW), F32)],
        compiler_params=_cp("parallel", "arbitrary"),
        name="gla",
    )(z3, zg3, wg, bg, s0, ng)


def _fox_decode_kernel(nps, pt_ref, q_ref, kn_ref, vn_ref, zgn_ref, *refs):
    k_refs = refs[0:nps]
    v_refs = refs[nps:2 * nps]
    lf_refs = refs[2 * nps:3 * nps]
    o_ref, lfo_ref = refs[3 * nps:3 * nps + 2]
    qbd_s, m_s, l_s, acc_s, cc_s = refs[3 * nps + 2:]
    st = pl.program_id(1)
    ns = pl.num_programs(1)
    P = PAGE_SIZE
    eye8 = _iota((8, 128), 0) == _iota((8, 128), 1)
    head8 = _iota((8, FOX_W), 0) == _iota((8, FOX_W), 1) // FOX_DH

    @pl.when(st == 0)
    def _():
        qbd = jnp.where(_iota((16, FOX_W), 0) == _iota((16, FOX_W), 1) // FOX_DH,
                        jnp.broadcast_to(q_ref[...], (16, FOX_W)), 0.0)
        qbd_s[...] = qbd.astype(BF16)
        m_s[...] = jnp.full_like(m_s, NEG_INF)
        l_s[...] = jnp.zeros_like(l_s)
        acc_s[...] = jnp.zeros_like(acc_s)
        cc_s[...] = jnp.zeros_like(cc_s)

    triu = (_iota((P, P), 0) <= _iota((P, P), 1)).astype(BF16)
    qbd = qbd_s[...]
    local = _dot_exact_r(jnp.concatenate([r[...] for r in lf_refs], axis=0), triu, 3)
    carry = cc_s[...]
    s_parts = []
    for i in range(nps):
        loc = local[8 * i:8 * (i + 1), :]
        s_parts.append(_dot(qbd, k_refs[i][...].astype(BF16))[0:8, :] - (loc + carry))
        carry = carry + loc[:, P - 1:P]
    cc_s[...] = carry
    m_loc = s_parts[0]
    for sp in s_parts[1:]:
        m_loc = jnp.maximum(m_loc, sp)
    m = m_s[...]
    m_new = jnp.maximum(m, jnp.max(m_loc, axis=1, keepdims=True))
    a = jnp.exp(m - m_new)
    m_s[...] = m_new
    p_parts = [jnp.exp(sp - m_new) for sp in s_parts]
    p_sum = p_parts[0]
    for pp in p_parts[1:]:
        p_sum = p_sum + pp
    l_s[...] = a * l_s[...] + jnp.sum(p_sum, axis=1, keepdims=True)
    o_even = jnp.zeros((16, FOX_W), F32)
    o_odd = jnp.zeros((16, FOX_W), F32)
    for i in range(0, nps, 2):
        pp = jnp.concatenate([p_parts[i], p_parts[i + 1]], axis=0).astype(BF16)
        o_even = o_even + _dot_nt(pp, v_refs[i][...].astype(BF16))
        o_odd = o_odd + _dot_nt(pp, v_refs[i + 1][...].astype(BF16))
    acc_s[...] = acc_s[...] * a + o_even[0:8, :] + o_odd[8:16, :]

    @pl.when(st == ns - 1)
    def _():
        lfn = _log_sigmoid(zgn_ref[...])
        lfo_ref[...] = lfn
        lfn_col = jnp.sum(jnp.where(eye8, jnp.broadcast_to(lfn, (8, 128)), 0.0), axis=1, keepdims=True)
        q8 = qbd_s[...].astype(F32)[0:8, :]
        s_n = jnp.sum(q8 * kn_ref[...], axis=1, keepdims=True) - (cc_s[...] + lfn_col)
        m = m_s[...]
        m_f = jnp.maximum(m, s_n)
        a = jnp.exp(m - m_f)
        p_n = jnp.exp(s_n - m_f)
        l_f = a * l_s[...] + p_n
        acc = (acc_s[...] * a + p_n * vn_ref[...]) / l_f
        o_ref[...] = jnp.sum(jnp.where(head8, acc, 0.0), axis=0, keepdims=True).astype(o_ref.dtype)


def fox_decode(page_table, q3, kn3, vn3, zgn3, pool_kT, pool_vT, pool_lfT, layer, n_pool, nps):
    DB, n_pages = page_table.shape
    base = layer * n_pool
    row = lambda n: pl.BlockSpec((None, 1, n), lambda b, s, pt: (b, 0, 0))

    def page_spec(i, r):
        return pl.BlockSpec((None, r, PAGE_SIZE), lambda b, s, pt: (base + pt[b, s * nps + i], 0, 0))

    in_specs = ([row(FOX_W), row(FOX_W), row(FOX_W), row(128)]
                + [page_spec(i, FOX_W) for i in range(nps)]
                + [page_spec(i, FOX_W) for i in range(nps)]
                + [page_spec(i, FOX_HEADS) for i in range(nps)])
    grid_spec = pltpu.PrefetchScalarGridSpec(
        num_scalar_prefetch=1,
        grid=(DB, n_pages // nps),
        in_specs=in_specs,
        out_specs=[row(FOX_W), row(128)],
        scratch_shapes=[pltpu.VMEM((16, FOX_W), BF16), pltpu.VMEM((8, 1), F32), pltpu.VMEM((8, 1), F32),
                        pltpu.VMEM((8, FOX_W), F32), pltpu.VMEM((8, 1), F32)])
    return pl.pallas_call(
        functools.partial(_fox_decode_kernel, nps),
        out_shape=[jax.ShapeDtypeStruct((DB, 1, FOX_W), BF16), jax.ShapeDtypeStruct((DB, 1, 128), F32)],
        grid_spec=grid_spec,
        compiler_params=_cp("parallel", "arbitrary"),
        name="fox_decode",
    )(page_table, q3, kn3, vn3, zgn3, *([pool_kT] * nps), *([pool_vT] * nps), *([pool_lfT] * nps))


def _outproj_kernel(yml_ref, ygla_ref, yfox_ref, h_ref, w_ref, g_ref, b_ref, wr_ref, br_ref, h1_ref, lg_ref):
    mix = (_dot(yml_ref[...], w_ref[0:ML_W, :]) + _dot(ygla_ref[...], w_ref[ML_W:ML_W + GLA_VW, :])
           + _dot(yfox_ref[...], w_ref[ML_W + GLA_VW:, :]))
    h1 = _ln_rows(DN_ALPHA * h_ref[...] + mix, g_ref[...], b_ref[...])
    h1_ref[...] = h1
    lg_ref[...] = _dot(h1.astype(BF16), wr_ref[...]) + br_ref[...]


def outproj(yml, ygla, yfox, h, w, g, b, wr, br, tm):
    T = h.shape[0]
    tok = lambda n: pl.BlockSpec((tm, n), lambda i: (i, 0))
    full = lambda r, n: pl.BlockSpec((r, n), lambda i: (0, 0))
    return pl.pallas_call(
        _outproj_kernel,
        out_shape=[jax.ShapeDtypeStruct((T, D_MODEL), F32), jax.ShapeDtypeStruct((T, 128), F32)],
        grid=(T // tm,),
        in_specs=[tok(ML_W), tok(GLA_VW), tok(FOX_W), tok(D_MODEL), full(D_MODEL, D_MODEL),
                  full(1, D_MODEL), full(1, D_MODEL), full(D_MODEL, 128), full(1, 128)],
        out_specs=[tok(D_MODEL), tok(128)],
        compiler_params=_cp("parallel"),
        name="outproj",
    )(yml, ygla, yfox, h, w, g, b, wr, br)


def _route_kernel(lg_ref, comb_ref):
    tm = lg_ref.shape[0]
    lt = lg_ref[...].T
    g = lt[_R_GRP:_R_GRP + N_GROUPS, :]
    e = lt[_R_EXP:_R_EXP + N_EXPERTS, :]
    gmax = jnp.max(g, axis=0, keepdims=True)
    gidx = _iota((N_GROUPS, tm), 0)
    g_sel = jnp.min(jnp.where(g == gmax, gidx, N_GROUPS), axis=0, keepdims=True)
    p_grp = 1.0 / jnp.sum(jnp.exp(g - gmax), axis=0, keepdims=True)
    eidx = _iota((N_EXPERTS, tm), 0)
    ev = jnp.where(eidx // EXPERTS_PER_GROUP == g_sel, e, NEG_INF)
    v1 = jnp.max(ev, axis=0, keepdims=True)
    i1 = jnp.min(jnp.where(ev == v1, eidx, N_EXPERTS), axis=0, keepdims=True)
    ev2 = jnp.where(eidx == i1, NEG_INF, ev)
    v2 = jnp.max(ev2, axis=0, keepdims=True)
    i2 = jnp.min(jnp.where(ev2 == v2, eidx, N_EXPERTS), axis=0, keepdims=True)
    t = jnp.exp(v2 - v1)
    w1 = p_grp / (1.0 + t)
    w2 = p_grp * t / (1.0 + t)
    comb = jnp.where(eidx == i1, w1, 0.0) + jnp.where(eidx == i2, w2, 0.0)
    comb128 = jnp.concatenate([comb, jnp.zeros((128 - N_EXPERTS, tm), F32)], axis=0)
    comb_ref[...] = comb128.T


def route(lg, tm):
    T = lg.shape[0]
    return pl.pallas_call(
        _route_kernel,
        out_shape=jax.ShapeDtypeStruct((T, 128), F32),
        grid=(T // tm,),
        in_specs=[pl.BlockSpec((tm, 128), lambda i: (i, 0))],
        out_specs=pl.BlockSpec((tm, 128), lambda i: (i, 0)),
        compiler_params=_cp("parallel"),
        name="route",
    )(lg)


def _moe_kernel(x_ref, comb_ref, wg_ref, wu_ref, wd_ref, g_ref, b_ref, o_ref, acc_s, xb_s):
    e = pl.program_id(1)

    @pl.when(e == 0)
    def _():
        acc_s[...] = jnp.zeros_like(acc_s)
        xb_s[...] = x_ref[...].astype(BF16)

    xb = xb_s[...]
    a = _dot(xb, wg_ref[...])
    u = _dot(xb, wu_ref[...])
    lane = _iota((1, 128), 1)
    ce = jnp.sum(jnp.where(lane == e, comb_ref[...], 0.0), axis=1, keepdims=True)
    hdn = (a * _sigmoid(a)) * u * ce
    acc_s[...] += _dot(hdn.astype(BF16), wd_ref[...])

    @pl.when(e == pl.num_programs(1) - 1)
    def _():
        o_ref[...] = _ln_rows(DN_ALPHA * x_ref[...] + acc_s[...], g_ref[...], b_ref[...])


def moe(x, comb, wg, wu, wd, g, b, tm):
    T = x.shape[0]
    return pl.pallas_call(
        _moe_kernel,
        out_shape=jax.ShapeDtypeStruct((T, D_MODEL), F32),
        grid=(T // tm, N_EXPERTS),
        in_specs=[pl.BlockSpec((tm, D_MODEL), lambda i, e: (i, 0)),
                  pl.BlockSpec((tm, 128), lambda i, e: (i, 0)),
                  pl.BlockSpec((None, D_MODEL, D_EXPERT), lambda i, e: (e, 0, 0)),
                  pl.BlockSpec((None, D_MODEL, D_EXPERT), lambda i, e: (e, 0, 0)),
                  pl.BlockSpec((None, D_EXPERT, D_MODEL), lambda i, e: (e, 0, 0)),
                  pl.BlockSpec((1, D_MODEL), lambda i, e: (0, 0)),
                  pl.BlockSpec((1, D_MODEL), lambda i, e: (0, 0))],
        out_specs=pl.BlockSpec((tm, D_MODEL), lambda i, e: (i, 0)),
        scratch_shapes=[pltpu.VMEM((tm, D_MODEL), F32), pltpu.VMEM((tm, D_MODEL), BF16)],
        compiler_params=_cp("parallel", "arbitrary"),
        name="moe",
    )(x, comb, wg, wu, wd, g, b)


def _perm_index():
    idx = []
    for name, n in (("ml_q", 256), ("ml_k", 256), ("ml_v", 256), ("ml_o", 256),
                    ("g_q", 128), ("g_k", 128), ("g_v", 256), ("g_r", 256),
                    ("fx_q", 512), ("fx_k", 512), ("fx_v", 512),
                    ("fx_f", 8), ("ml_i", 4), ("ml_f", 4), ("g_lr", 16)):
        idx.extend(range(_OFF[name], _OFF[name] + n))
    return np.asarray(idx, np.int32)


def _col_scale():
    s = np.ones((_NPAD,), np.float32)
    s[_ML0 + ML_W:_ML0 + 2 * ML_W] = ML_DH ** -0.5
    s[_FQ0:_FK0] = FOX_DH ** -0.5
    return s


def _prep_inproj(w_in, b_in):
    idx = _perm_index()
    pad = _NPAD - idx.shape[0]
    scale = jnp.asarray(_col_scale())
    w = jnp.pad(jnp.take(w_in, idx, axis=2), ((0, 0), (0, 0), (0, pad))) * scale
    b = jnp.pad(jnp.take(b_in, idx, axis=1), ((0, 0), (0, pad))) * scale
    return w.astype(BF16), b.reshape(DEPTH, 1, _NPAD)


def _prep_inproj_t(w_in, b_in):
    wt = jnp.transpose(w_in, (0, 2, 1))
    parts, bparts = [], []
    for name, sc in (("fx_q", FOX_DH ** -0.5 * LOG2E), ("fx_k", 1.0), ("fx_v", 1.0)):
        parts.append(wt[:, _OFF[name]:_OFF[name] + FOX_W, :] * sc)
        bparts.append(b_in[:, _OFF[name]:_OFF[name] + FOX_W] * sc)
    return jnp.concatenate(parts, axis=1).astype(BF16), jnp.concatenate(bparts, axis=1)[:, :, None]


def _block_diag_tile(x, reps, blk_r, blk_c):
    B, R, _ = x.shape
    t = jnp.tile(x, (1, 1, reps))
    keep = (np.arange(R)[:, None] // blk_r) == (np.arange(reps * blk_c)[None, :] // blk_c)
    return jnp.where(jnp.asarray(keep), t, 0.0)


def _diag_blocks(x, heads, blk_r, blk_c):
    B = x.shape[0]
    x5 = x.reshape(B, heads, blk_r, heads, blk_c)
    return jnp.stack([x5[:, h, :, h, :] for h in range(heads)], axis=1)


def _pad_chunk(x):
    return jnp.pad(x[:, None, :], ((0, 0), (0, CHUNK - 1), (0, 0)))


def kernel(x_prompt, x_sample, cache_fox_k, cache_fox_v, cache_fox_logf, state_mlstm_C, state_mlstm_n, state_mlstm_m, state_gla_S, page_table, ln_in_g, ln_in_b, w_in, b_in, mlstm_norm_g, gla_w_gate_up, gla_b_gate, gla_norm_g, w_out, ln1_g, ln1_b, w_router_group, b_router_group, w_router_expert, b_router_expert, w_exp_gate, w_exp_up, w_exp_down, ln2_g, ln2_b):
    B, S, D = x_prompt.shape
    DB = x_sample.shape[0]
    T = B * S
    n_pool = cache_fox_k.shape[1]
    TS = 128

    w_in_p, b_in_p = _prep_inproj(w_in, b_in)
    w_out_b = w_out.astype(BF16)
    wr = jnp.zeros((DEPTH, D, 128), F32)
    wr = wr.at[:, :, _R_GRP:_R_GRP + N_GROUPS].set(w_router_group).at[:, :, _R_EXP:_R_EXP + N_EXPERTS].set(w_router_expert)
    wr = wr.astype(BF16)
    br = jnp.zeros((DEPTH, 1, 128), F32)
    br = br.at[:, 0, _R_GRP:_R_GRP + N_GROUPS].set(b_router_group).at[:, 0, _R_EXP:_R_EXP + N_EXPERTS].set(b_router_expert)
    wg_b = w_exp_gate.astype(BF16)
    wu_b = w_exp_up.astype(BF16)
    wd_b = w_exp_down.astype(BF16)
    wgl = jnp.zeros((DEPTH, 128, GLA_KW), F32).at[:, _G_LR:_G_LR + GLA_RANK, :].set(gla_w_gate_up).astype(BF16)
    row = lambda a: a.reshape(DEPTH, 1, -1)
    ln1g, ln1b, ln2g, ln2b = row(ln1_g), row(ln1_b), row(ln2_g), row(ln2_b)
    mlng, glng, glbg = row(mlstm_norm_g), row(gla_norm_g), row(gla_b_gate)
    w_in_t, b_in_t = _prep_inproj_t(w_in, b_in)
    pool_k = jnp.transpose(cache_fox_k, (0, 1, 3, 4, 2)).reshape(DEPTH * n_pool, FOX_W, PAGE_SIZE)
    pool_v = jnp.transpose(cache_fox_v, (0, 1, 3, 4, 2)).reshape(DEPTH * n_pool, FOX_W, PAGE_SIZE)
    pool_lf = jnp.transpose(cache_fox_logf, (0, 1, 3, 2)).reshape(DEPTH * n_pool, FOX_HEADS, PAGE_SIZE)

    g_in, b_in_ln = ln_in_g.reshape(1, D), ln_in_b.reshape(1, D)
    hp = ln_rows(x_prompt.reshape(T, D), g_in, b_in_ln, 512)
    xs = jnp.pad(x_sample.reshape(DB, D), ((0, TS - DB), (0, 0)))
    hs = ln_rows(xs, g_in, b_in_ln, TS)

    zc = jnp.zeros((B, ML_W, ML_W), F32)
    zn = jnp.zeros((B, 1, ML_W), F32)
    zm = jnp.zeros((B, 1, 128), F32)
    zs = jnp.zeros((B, GLA_KW, GLA_VW), F32)

    outs = {k: [] for k in ("lfp", "ks", "vs", "lfs", "cp", "np", "mp", "cs", "ns", "ms", "gp", "gs")}
    kv_all = None
    tq = min(FOX_TQ, S)
    for l in range(DEPTH):
        zml, zgla, zg, k_rm, qT, kT_all, vT_all = inproj_t(hp, w_in_p[l], b_in_p[l], w_in_t[l], b_in_t[l],
                                                           kv_all, l, B, S, min(INPROJ_TM, S))
        kv_all = (kT_all, vT_all)
        zg3 = zg.reshape(B, S, 128)
        lf, c_sh = foxgate(zg3)
        y_fox = fox_attn(qT, k_rm, vT_all, c_sh, l, B, S, tq)
        y_ml, c_o, n_o, m_o = mlstm(zml.reshape(B, S, 4 * ML_W), zg3, zc, zn, zm, mlng[l], CHUNK, min(MLSTM_BB, B))
        y_gla, s_o = gla(zgla.reshape(B, S, -1), zg3, wgl[l], glbg[l], zs, glng[l], CHUNK, min(GLA_BB, B))
        h1, lg = outproj(y_ml.reshape(T, ML_W), y_gla.reshape(T, GLA_VW), y_fox, hp, w_out_b[l],
                         ln1g[l], ln1b[l], wr[l], br[l], 512)
        comb = route(lg, 512)
        hp = moe(h1, comb, wg_b[l], wu_b[l], wd_b[l], ln2g[l], ln2b[l], min(1024, T))
        outs["lfp"].append(lf)
        outs["cp"].append(_diag_blocks(c_o, ML_HEADS, ML_DH, ML_DH))
        outs["np"].append(n_o.reshape(B, ML_HEADS, ML_DH)); outs["mp"].append(m_o[:, 0, :ML_HEADS])
        outs["gp"].append(_diag_blocks(s_o, GLA_HEADS, GLA_DK, GLA_DV))

        zml, zgla, fq, fk, fv, zg = inproj(hs, w_in_p[l], b_in_p[l], TS)
        zg3 = _pad_chunk(zg[:DB])
        c0 = _block_diag_tile(state_mlstm_C[l].astype(F32).reshape(DB, ML_W, ML_DH), ML_HEADS, ML_DH, ML_DH)
        n0 = state_mlstm_n[l].astype(F32).reshape(DB, 1, ML_W)
        m0 = jnp.pad(state_mlstm_m[l].astype(F32), ((0, 0), (0, 128 - ML_HEADS))).reshape(DB, 1, 128)
        s0 = _block_diag_tile(state_gla_S[l].astype(F32).reshape(DB, GLA_KW, GLA_DV), GLA_HEADS, GLA_DK, GLA_DV)
        y_ml, c_o, n_o, m_o = mlstm(_pad_chunk(zml[:DB]), zg3, c0, n0, m0, mlng[l], 1, min(SAMPLE_BB, DB))
        y_gla, s_o = gla(_pad_chunk(zgla[:DB]), zg3, wgl[l], glbg[l], s0, glng[l], 1, min(SAMPLE_BB, DB))
        y_fox, lfn = fox_decode(page_table, fq[:DB].astype(F32).reshape(DB, 1, FOX_W),
                                fk[:DB].reshape(DB, 1, FOX_W), fv[:DB].reshape(DB, 1, FOX_W),
                                zg[:DB].reshape(DB, 1, 128), pool_k, pool_v, pool_lf, l, n_pool,
                                min(16, page_table.shape[1]))
        padr = lambda a: jnp.pad(a, ((0, TS - DB), (0, 0)))
        h1, lg = outproj(padr(y_ml[:, 0, :]), padr(y_gla[:, 0, :]), padr(y_fox.reshape(DB, FOX_W)), hs, w_out_b[l],
                         ln1g[l], ln1b[l], wr[l], br[l], TS)
        comb = route(lg, TS)
        hs = moe(h1, comb, wg_b[l], wu_b[l], wd_b[l], ln2g[l], ln2b[l], TS)
        outs["ks"].append(fk[:DB]); outs["vs"].append(fv[:DB]); outs["lfs"].append(lfn[:, 0, :FOX_HEADS])
        outs["cs"].append(_diag_blocks(c_o, ML_HEADS, ML_DH, ML_DH))
        outs["ns"].append(n_o.reshape(DB, ML_HEADS, ML_DH)); outs["ms"].append(m_o[:, 0, :ML_HEADS])
        outs["gs"].append(_diag_blocks(s_o, GLA_HEADS, GLA_DK, GLA_DV))

    st = lambda k, shape, dt: jnp.stack(outs[k]).reshape((DEPTH,) + shape).astype(dt)
    kd, vd, ld = cache_fox_k.dtype, cache_fox_v.dtype, cache_fox_logf.dtype
    cd, nd, md, sd = state_mlstm_C.dtype, state_mlstm_n.dtype, state_mlstm_m.dtype, state_gla_S.dtype
    kvp = lambda a, dt: jnp.transpose(a.reshape(DEPTH, B, FOX_HEADS, FOX_DH, S), (0, 1, 4, 2, 3)).astype(dt)
    lfp = jnp.transpose(jnp.stack(outs["lfp"]), (0, 1, 3, 2)).astype(ld)
    return (hp.reshape(B, S, D), hs[:DB].reshape(DB, 1, D),
            kvp(kv_all[0], kd), kvp(kv_all[1], vd), lfp,
            st("ks", (DB, 1, FOX_HEADS, FOX_DH), kd), st("vs", (DB, 1, FOX_HEADS, FOX_DH), vd),
            st("lfs", (DB, 1, FOX_HEADS), ld),
            st("cp", (B, ML_HEADS, ML_DH, ML_DH), cd), st("np", (B, ML_HEADS, ML_DH), nd), st("mp", (B, ML_HEADS), md),
            st("cs", (DB, ML_HEADS, ML_DH, ML_DH), cd), st("ns", (DB, ML_HEADS, ML_DH), nd), st("ms", (DB, ML_HEADS), md),
            st("gp", (B, GLA_HEADS, GLA_DK, GLA_DV), sd), st("gs", (DB, GLA_HEADS, GLA_DK, GLA_DV), sd))
```

```python
import functools

import numpy as np
import jax
import jax.numpy as jnp
from jax import lax
from jax.experimental import pallas as pl
from jax.experimental.pallas import tpu as pltpu

F32 = jnp.float32
BF16 = jnp.bfloat16
HI = lax.Precision.HIGHEST

D_MODEL = 1024
DEPTH = 4
PAGE_SIZE = 128
ML_DH = 64
ML_HEADS = 4
ML_W = 256
GLA_DK = 32
GLA_DV = 64
GLA_HEADS = 4
GLA_KW = 128
GLA_VW = 256
GLA_RANK = 16
GLA_TAU = 16.0
FOX_DH = 64
FOX_HEADS = 8
FOX_W = 512
CHUNK = 128
SUB = 16
N_GROUPS = 4
EXPERTS_PER_GROUP = 4
N_EXPERTS = 16
D_EXPERT = 256
DN_ALPHA = (2.0 * DEPTH) ** 0.25
LN_EPS = 1e-5
NORM_EPS = 1e-6
NEG_INF = float("-inf")
LOG2E = 1.4426950408889634

_OFF = dict(ml_q=0, ml_k=256, ml_v=512, ml_o=768, ml_i=1024, ml_f=1028, g_q=1032, g_k=1160, g_v=1288,
            g_r=1544, g_lr=1800, fx_q=1816, fx_k=2328, fx_v=2840, fx_f=3352)
_ML0, _GLA0, _FQ0, _FK0, _FV0, _ZG0, _NPAD = 0, 1024, 1792, 2304, 2816, 3328, 3456
_G_FXF, _G_MLI, _G_MLF, _G_LR = 0, 8, 12, 16
_R_EXP, _R_GRP = 0, 16

VMEM_LIMIT = 56 * 1024 * 1024
FOX_TQ = 1024
INPROJ_TM = 512
SAMPLE_BB = 8
FOX_KS, FOX_QS = 128, 256
FOX_LOOKAHEAD = 8
MLSTM_BB, GLA_BB = 4, 4


def _cp(*sem):
    return pltpu.CompilerParams(dimension_semantics=sem, vmem_limit_bytes=VMEM_LIMIT)


def _dot(a, b, precision=None):
    return jnp.dot(a, b, preferred_element_type=F32, precision=precision)


def _dot_nt(a, b):
    return lax.dot_general(a, b, (((1,), (1,)), ((), ())), preferred_element_type=F32)


def _dot_tn(a, b):
    return lax.dot_general(a, b, (((0,), (0,)), ((), ())), preferred_element_type=F32)


def _split_bf16(x, terms):
    parts, r = [], x
    for _ in range(terms):
        p = r.astype(BF16)
        parts.append(p)
        r = r - p.astype(F32)
    return parts


def _dot_exact_l(sel, x, terms):
    out = None
    for p in _split_bf16(x, terms):
        d = _dot(sel, p)
        out = d if out is None else out + d
    return out


def _dot_exact_r(x, sel, terms):
    out = None
    for p in _split_bf16(x, terms):
        d = _dot(p, sel)
        out = d if out is None else out + d
    return out


def _log_sigmoid(x):
    return jnp.minimum(x, 0.0) - jnp.log1p(jnp.exp(-jnp.abs(x)))


def _sigmoid(x):
    return 1.0 / (1.0 + jnp.exp(-x))


def _ln_rows(x, g, b):
    mu = jnp.mean(x, axis=-1, keepdims=True)
    xc = x - mu
    var = jnp.mean(xc * xc, axis=-1, keepdims=True)
    return xc * lax.rsqrt(var + LN_EPS) * g + b


def _iota(shape, dim):
    return lax.broadcasted_iota(jnp.int32, shape, dim)


def _ln_kernel(x_ref, g_ref, b_ref, o_ref):
    o_ref[...] = _ln_rows(x_ref[...], g_ref[...], b_ref[...])


def ln_rows(x, g, b, tm):
    T, D = x.shape
    return pl.pallas_call(
        _ln_kernel,
        out_shape=jax.ShapeDtypeStruct((T, D), F32),
        grid=(T // tm,),
        in_specs=[pl.BlockSpec((tm, D), lambda i: (i, 0)),
                  pl.BlockSpec((1, D), lambda i: (0, 0)),
                  pl.BlockSpec((1, D), lambda i: (0, 0))],
        out_specs=pl.BlockSpec((tm, D), lambda i: (i, 0)),
        compiler_params=_cp("parallel"),
        name="ln_in",
    )(x, g, b)


_IN_BOUNDS = ((_ML0, _GLA0), (_GLA0, _FQ0), (_FQ0, _FK0), (_FK0, _FV0), (_FV0, _ZG0), (_ZG0, _NPAD))


def _inproj_kernel(h_ref, w_ref, b_ref, *out_refs):
    xb = h_ref[...].astype(BF16)
    for ref, (a, b) in zip(out_refs, _IN_BOUNDS):
        ref[...] = (_dot(xb, w_ref[:, a:b]) + b_ref[:, a:b]).astype(ref.dtype)


def inproj(h, w, b, tm):
    T = h.shape[0]
    widths = [b_ - a_ for a_, b_ in _IN_BOUNDS]
    dtypes = [F32, F32, BF16, F32, F32, F32]
    return pl.pallas_call(
        _inproj_kernel,
        out_shape=[jax.ShapeDtypeStruct((T, n), dt) for n, dt in zip(widths, dtypes)],
        grid=(T // tm,),
        in_specs=[pl.BlockSpec((tm, D_MODEL), lambda i: (i, 0)),
                  pl.BlockSpec((D_MODEL, _NPAD), lambda i: (0, 0)),
                  pl.BlockSpec((1, _NPAD), lambda i: (0, 0))],
        out_specs=[pl.BlockSpec((tm, n), lambda i: (i, 0)) for n in widths],
        compiler_params=_cp("parallel"),
        name="inproj",
    )(h, w, b)


def _inproj_t_kernel(h_ref, w_ref, b_ref, wt_ref, bt_ref, *refs):
    ml_ref, gla_ref, zg_ref, krm_ref, qT_ref, kT_ref, vT_ref = refs[-7:]
    xb = h_ref[...].astype(BF16)
    for ref, (a, b) in ((ml_ref, (_ML0, _GLA0)), (gla_ref, (_GLA0, _FQ0)), (zg_ref, (_ZG0, _NPAD)),
                        (krm_ref, (_FK0, _FV0))):
        ref[...] = (_dot(xb, w_ref[:, a:b]) + b_ref[:, a:b]).astype(ref.dtype)
    for t, ref in enumerate((qT_ref, kT_ref, vT_ref)):
        sl = slice(t * FOX_W, (t + 1) * FOX_W)
        ref[...] = (_dot_nt(wt_ref[sl, :], xb) + bt_ref[sl, :]).astype(ref.dtype)


def inproj_t(h, w, b, wt, bt, kv_all, layer, B, S, tm):
    T = h.shape[0]
    nb = S // tm
    tok = lambda n: pl.BlockSpec((tm, n), lambda i: (i, 0))
    full = lambda r, n: pl.BlockSpec((r, n), lambda i: (0, 0))
    kv_spec = pl.BlockSpec((None, None, FOX_W, tm), lambda i: (layer, i // nb, 0, i % nb))
    kv_shape = jax.ShapeDtypeStruct((DEPTH, B, FOX_W, S), F32)
    in_specs = [tok(D_MODEL), full(D_MODEL, _NPAD), full(1, _NPAD), full(3 * FOX_W, D_MODEL), full(3 * FOX_W, 1)]
    args = [h, w, b, wt, bt]
    aliases = {}
    if kv_all is not None:
        in_specs += [pl.BlockSpec(memory_space=pl.ANY), pl.BlockSpec(memory_space=pl.ANY)]
        args += list(kv_all)
        aliases = {5: 5, 6: 6}
    return pl.pallas_call(
        _inproj_t_kernel,
        out_shape=[jax.ShapeDtypeStruct((T, 4 * ML_W), F32), jax.ShapeDtypeStruct((T, _FQ0 - _GLA0), F32),
                   jax.ShapeDtypeStruct((T, 128), F32), jax.ShapeDtypeStruct((T, FOX_W), BF16),
                   jax.ShapeDtypeStruct((B, FOX_W, S), BF16), kv_shape, kv_shape],
        grid=(T // tm,),
        in_specs=in_specs,
        out_specs=[tok(4 * ML_W), tok(_FQ0 - _GLA0), tok(128), tok(FOX_W),
                   pl.BlockSpec((None, FOX_W, tm), lambda i: (i // nb, 0, i % nb)), kv_spec, kv_spec],
        input_output_aliases=aliases,
        compiler_params=_cp("arbitrary"),
        name="inproj_t",
    )(*args)


def _foxgate_kernel(zg_ref, lft_ref, c_ref):
    S = zg_ref.shape[0]
    tri = (_iota((CHUNK, CHUNK), 0) >= _iota((CHUNK, CHUNK), 1)).astype(BF16)
    src = _iota((128, FOX_W), 0)
    dst = _iota((128, FOX_W), 1)
    place = [jnp.where((dst == 128 * (src // 2) + 3 * (src % 2) + t) & (src < FOX_HEADS), 1.0, 0.0).astype(BF16)
             for t in range(3)]
    carry = jnp.zeros((1, 128), F32)
    for j in range(S // CHUNK):
        sl = slice(j * CHUNK, (j + 1) * CHUNK)
        lf = _log_sigmoid(zg_ref[sl, :])
        lft_ref[:, sl] = lf.T[_G_FXF:_G_FXF + FOX_HEADS, :]
        cs = _dot_exact_l(tri, lf, 3) + carry
        carry = cs[CHUNK - 1:CHUNK, :]
        out = None
        for part, pl_t in zip(_split_bf16(cs * LOG2E, 3), place):
            d = _dot(part, pl_t)
            out = d if out is None else out + d
        c_ref[sl, :] = out.astype(BF16)


def foxgate(zg3):
    B, S, _ = zg3.shape
    return pl.pallas_call(
        _foxgate_kernel,
        out_shape=[jax.ShapeDtypeStruct((B, FOX_HEADS, S), F32), jax.ShapeDtypeStruct((B * S, FOX_W), BF16)],
        grid=(B,),
        in_specs=[pl.BlockSpec((None, S, 128), lambda b: (b, 0, 0))],
        out_specs=[pl.BlockSpec((None, FOX_HEADS, S), lambda b: (b, 0, 0)),
                   pl.BlockSpec((S, FOX_W), lambda b: (b, 0))],
        compiler_params=_cp("parallel"),
        name="foxgate",
    )(zg3)


def _fox_attn_kernel(tq, qT_ref, k_ref, vT_ref, c_ref, o_ref, vb_s):
    qi = pl.program_id(2)

    @pl.when(qi == 0)
    def _():
        vb_s[...] = vT_ref[...].astype(BF16)

    qT = qT_ref[...]
    rowp = _iota((128, 1), 0)
    zero = jnp.zeros_like(qT)
    q_head = (jnp.where(rowp < FOX_DH, qT, zero), jnp.where(rowp >= FOX_DH, qT, zero))
    qTh = tuple(jnp.concatenate(
        [q_head[hh], jnp.broadcast_to(jnp.where(rowp // 3 == hh, -1.0, 0.0), qT.shape).astype(BF16)], axis=0)
        for hh in range(2))
    ks, qs = min(FOX_KS, tq), min(FOX_QS, tq)
    nqs = tq // qs
    tri_mask = _iota((ks, qs), 0) - _iota((ks, qs), 1)

    def block(j, carry, masked):
        off = pl.multiple_of(j * tq, tq)
        carry = list(carry)
        tiles = [(kk, hh, t) for kk in range(tq // ks) for hh in range(2) for t in range(nqs)
                 if not (masked and kk * ks >= (t + 1) * qs)]

        def scores(kk, hh, t):
            ko = pl.multiple_of(off + kk * ks, ks)
            kc = jnp.concatenate([k_ref[pl.ds(ko, ks), :], c_ref[pl.ds(ko, ks), :]], axis=1)
            s = _dot(kc, qTh[hh][:, t * qs:(t + 1) * qs])
            if masked and (kk + 1) * ks - 1 > t * qs:
                s = jnp.where(tri_mask <= t * qs - kk * ks, s, NEG_INF)
            return s

        ready = [scores(*tl) for tl in tiles[:FOX_LOOKAHEAD]]
        for i, (kk, hh, t) in enumerate(tiles):
            if i + FOX_LOOKAHEAD < len(tiles):
                ready.append(scores(*tiles[i + FOX_LOOKAHEAD]))
            s = ready[i]
            ko = pl.multiple_of(off + kk * ks, ks)
            m, l, acc = carry[hh * nqs + t]
            m_new = jnp.maximum(m, jnp.max(s, axis=0, keepdims=True))
            a = jnp.exp2(m - m_new)
            p = jnp.exp2(s - m_new)
            l = a * l + jnp.sum(p, axis=0, keepdims=True)
            vb = vb_s[hh * FOX_DH:(hh + 1) * FOX_DH, pl.ds(ko, ks)]
            acc = a * acc + _dot(vb, p.astype(BF16))
            carry[hh * nqs + t] = (m_new, l, acc)
        return tuple(carry)

    init = tuple((jnp.full((1, qs), NEG_INF, F32), jnp.zeros((1, qs), F32), jnp.zeros((FOX_DH, qs), F32))
                 for _ in range(2 * nqs))
    carry = lax.fori_loop(0, qi, functools.partial(block, masked=False), init)
    carry = block(qi, carry, True)
    oT = jnp.concatenate(
        [jnp.concatenate([carry[hh * nqs + t][2] / carry[hh * nqs + t][1] for t in range(nqs)], axis=1)
         for hh in range(2)], axis=0)
    o_ref[...] = oT.T.astype(o_ref.dtype)


def fox_attn(qT, k_rm, vT_all, c_sh, layer, B, S, tq):
    T = B * S
    nq = S // tq
    return pl.pallas_call(
        functools.partial(_fox_attn_kernel, tq),
        out_shape=jax.ShapeDtypeStruct((T, FOX_W), BF16),
        grid=(B, FOX_HEADS // 2, nq),
        in_specs=[pl.BlockSpec((None, 128, tq), lambda b, hp, qi: (b, hp, qi)),
                  pl.BlockSpec((S, 128), lambda b, hp, qi: (b, hp)),
                  pl.BlockSpec((None, None, 128, S), lambda b, hp, qi: (layer, b, hp, 0)),
                  pl.BlockSpec((S, 128), lambda b, hp, qi: (b, hp))],
        out_specs=pl.BlockSpec((tq, 128), lambda b, hp, qi: (b * nq + qi, hp)),
        scratch_shapes=[pltpu.VMEM((128, S), BF16)],
        compiler_params=_cp("parallel", "arbitrary", "arbitrary"),
        name="fox_attn",
    )(qT, k_rm, vT_all, c_sh)


def _head_tile(dv, heads):
    w = heads * dv
    tile = jnp.where(_iota((dv, w), 1) % dv == _iota((dv, w), 0), 1.0, 0.0).astype(BF16)
    gather = jnp.where(_iota((w, dv), 0) % dv == _iota((w, dv), 1), 1.0, 0.0).astype(BF16)
    return tile, gather


def _mlstm_kernel(n_valid, bb, single, has_state, z_ref, zg_ref, *refs):
    if has_state:
        c0_ref, n0_ref, m0_ref = refs[:3]
        refs = refs[3:]
    ng_ref, y_ref, co_ref, no_ref, mo_ref, c_s, n_s, m_s = refs
    c = pl.program_id(1)
    nc = pl.num_programs(1)
    L = CHUNK
    bd = _iota((ML_W, ML_W), 0) // ML_DH == _iota((ML_W, ML_W), 1) // ML_DH
    tile_m, gather_m = _head_tile(ML_DH, ML_HEADS)

    @pl.when(c == 0)
    def _():
        if has_state:
            for i in range(bb):
                c_s[i] = jnp.where(bd, _dot_exact_r(c0_ref[i], tile_m, 3), 0.0)
            n_s[...] = n0_ref[...]
            m_s[...] = m0_ref[...]
        else:
            c_s[...] = jnp.zeros_like(c_s)
            n_s[...] = jnp.zeros_like(n_s)
            m_s[...] = jnp.zeros_like(m_s)

    row = _iota((L, L), 0)
    col = _iota((L, L), 1)
    causal = row >= col
    tri = causal.astype(BF16)
    lane256 = _iota((1, ML_W), 1)
    lane128 = _iota((1, 128), 1)
    rowi = _iota((L, 1), 0)
    valid = rowi < n_valid

    def chunk_rows(ref, i):
        if single:
            return jnp.where(rowi == 0, jnp.broadcast_to(ref[i:i + 1, :], (L, ref.shape[1])), 0.0)
        return ref[i]

    seg = (_iota((ML_W, 128), 0) // ML_DH == _iota((ML_W, 128), 1)).astype(BF16)
    segmean = (bd.astype(F32) * (1.0 / ML_DH)).astype(BF16)
    ng = ng_ref[...]

    elems = range(bb)
    units = [(i, h) for i in elems for h in range(ML_HEADS)]
    hms = [lane256 // ML_DH == h for h in range(ML_HEADS)]

    E = []
    for i in elems:
        z = chunk_rows(z_ref, i)
        q = z[:, 0:ML_W]
        k = z[:, ML_W:2 * ML_W]
        v = z[:, 2 * ML_W:3 * ML_W]
        g = chunk_rows(zg_ref, i)
        li_all = g
        lf_all = _log_sigmoid(g)
        if n_valid < L:
            li_all = jnp.where(valid, li_all, NEG_INF)
            lf_all = jnp.where(valid, lf_all, 0.0)
        E.append(dict(q=q, k=k, li_all=li_all, lf_all=lf_all, kb=k.astype(BF16), vb=v.astype(BF16),
                      og=z[:, 3 * ML_W:4 * ML_W], C=c_s[i], n_row=n_s[i], m_row=m_s[i]))
    for e in E:
        e["b_all"] = _dot_exact_l(tri, e["lf_all"], 3)
    for e in E:
        e["qC"] = _dot(e["q"].astype(BF16), e["C"].astype(BF16))
    for e in E:
        e["qn"] = _dot_exact_r(e["q"] * e["n_row"], seg, 2)
    for e in E:
        e["liT"] = e["li_all"].T
        e["bT"] = e["b_all"].T

    U = {}
    for (i, h) in units:
        e = E[i]
        U[i, h] = dict(qk=_dot_nt(jnp.where(hms[h], e["q"], 0.0).astype(BF16), e["kb"]))

    for (i, h) in units:
        e, u = E[i], U[i, h]
        b_col = e["b_all"][:, _G_MLF + h:_G_MLF + h + 1]
        b_row = e["bT"][_G_MLF + h:_G_MLF + h + 1, :]
        li_row = e["liT"][_G_MLI + h:_G_MLI + h + 1, :]
        m_prev = e["m_row"][:, h:h + 1]
        dmat = jnp.where(causal, b_col - b_row + li_row, NEG_INF)
        inter = b_col + m_prev
        m_t = jnp.maximum(inter, jnp.max(dmat, axis=1, keepdims=True))
        u.update(b_col=b_col, m_prev=m_prev, m_t=m_t, w_prev=jnp.exp(inter - m_t))
        u["s"] = u["qk"] * jnp.exp(dmat - m_t)

    for (i, h) in units:
        U[i, h]["num"] = _dot(U[i, h]["s"].astype(BF16), E[i]["vb"])

    for e in E:
        e.update(h_acc=jnp.zeros((L, ML_W), F32), kg=jnp.zeros((L, ML_W), F32),
                 gp_row=jnp.zeros((1, ML_W), F32), m_new_row=e["m_row"])
    for (i, h) in units:
        e, u = E[i], U[i, h]
        m_t, w_prev, b_col = u["m_t"], u["w_prev"], u["b_col"]
        den = jnp.sum(u["s"], axis=1, keepdims=True) + w_prev * e["qn"][:, h:h + 1]
        den = jnp.maximum(jnp.abs(den), jnp.exp(-m_t))
        e["h_acc"] = jnp.where(hms[h], (u["num"] + e["qC"] * w_prev) / den, e["h_acc"])
        b_last = b_col[L - 1:L, :]
        m_new = m_t[L - 1:L, :]
        li_col = e["li_all"][:, _G_MLI + h:_G_MLI + h + 1]
        g_rows = jnp.exp(b_last - b_col + li_col - m_new)
        e["kg"] = jnp.where(hms[h], e["k"] * g_rows, e["kg"])
        e["gp_row"] = jnp.where(hms[h], jnp.exp(b_last + u["m_prev"] - m_new), e["gp_row"])
        e["m_new_row"] = jnp.where(lane128 == h, m_new, e["m_new_row"])

    for e in E:
        e["upd"] = _dot_tn(e["kg"].astype(BF16), e["vb"])
    for e in E:
        e["ms"] = _dot_exact_r(e["h_acc"] * e["h_acc"], segmean, 2)
    for i, e in zip(elems, E):
        c_s[i] = e["C"] * e["gp_row"] + jnp.where(bd, e["upd"], 0.0)
        n_s[i] = e["n_row"] * e["gp_row"] + jnp.sum(e["kg"], axis=0, keepdims=True)
        m_s[i] = e["m_new_row"]
        y = e["h_acc"] * lax.rsqrt(e["ms"] + NORM_EPS) * ng * _sigmoid(e["og"])
        if single:
            y_ref[i:i + 1, :] = y[0:1, :].astype(y_ref.dtype)
        else:
            y_ref[i] = y.astype(y_ref.dtype)

    @pl.when(c == nc - 1)
    def _():
        for i in range(bb):
            co_ref[i] = _dot_exact_r(c_s[i], gather_m, 3)
        no_ref[...] = n_s[...]
        mo_ref[...] = m_s[...]


def mlstm(z, zg, state, ng, bb):
    single = z.ndim == 2
    B = z.shape[0]
    nc = 1 if single else z.shape[1] // CHUNK
    st = lambda shape: pl.BlockSpec((bb,) + shape, lambda b, c: (b, 0, 0))
    if single:
        tok = lambda n: pl.BlockSpec((bb, n), lambda b, c: (b, 0))
        y_shape = jax.ShapeDtypeStruct((B, ML_W), F32)
    else:
        tok = lambda n: pl.BlockSpec((bb, CHUNK, n), lambda b, c: (b, c, 0))
        y_shape = jax.ShapeDtypeStruct((B, z.shape[1], ML_W), BF16)
    state_specs = [st((ML_W, ML_DH)), st((1, ML_W)), st((1, 128))]
    has_state = state is not None
    return pl.pallas_call(
        functools.partial(_mlstm_kernel, 1 if single else CHUNK, bb, single, has_state),
        out_shape=[y_shape, jax.ShapeDtypeStruct((B, ML_W, ML_DH), F32),
                   jax.ShapeDtypeStruct((B, 1, ML_W), F32), jax.ShapeDtypeStruct((B, 1, 128), F32)],
        grid=(B // bb, nc),
        in_specs=[tok(4 * ML_W), tok(128)] + (state_specs if has_state else [])
                 + [pl.BlockSpec((1, ML_W), lambda b, c: (0, 0))],
        out_specs=[tok(ML_W)] + state_specs,
        scratch_shapes=[pltpu.VMEM((bb, ML_W, ML_W), F32), pltpu.VMEM((bb, 1, ML_W), F32),
                        pltpu.VMEM((bb, 1, 128), F32)],
        compiler_params=_cp("parallel", "arbitrary"),
        name="mlstm",
    )(z, zg, *(state if has_state else ()), ng)


def _gla_kernel(n_valid, bb, single, has_state, z_ref, zg_ref, wg_ref, bg_ref, *refs):
    if has_state:
        s0_ref = refs[0]
        refs = refs[1:]
    ng_ref, y_ref, so_ref, s_s, o_s = refs
    c = pl.program_id(1)
    nc = pl.num_programs(1)
    L = CHUNK
    bd = _iota((GLA_KW, GLA_VW), 0) // GLA_DK == _iota((GLA_KW, GLA_VW), 1) // GLA_DV
    tile_m, gather_m = _head_tile(GLA_DV, GLA_HEADS)

    @pl.when(c == 0)
    def _():
        if has_state:
            for i in range(bb):
                s_s[i] = jnp.where(bd, _dot_exact_r(s0_ref[i], tile_m, 3), 0.0)
        else:
            s_s[...] = jnp.zeros_like(s_s)

    tri = (_iota((L, L), 0) >= _iota((L, L), 1)).astype(BF16)
    lane128 = _iota((1, GLA_KW), 1)
    lane256 = _iota((1, GLA_VW), 1)
    rowi = _iota((L, 1), 0)
    valid = rowi < n_valid

    def chunk_rows(ref, i):
        if single:
            return jnp.where(rowi == 0, jnp.broadcast_to(ref[i:i + 1, :], (L, ref.shape[1])), 0.0)
        return ref[i]

    segexp = jnp.where(bd, 1.0, 0.0).astype(BF16)
    bdv = _iota((GLA_VW, GLA_VW), 0) // GLA_DV == _iota((GLA_VW, GLA_VW), 1) // GLA_DV
    segmean = (bdv.astype(F32) * (1.0 / GLA_DV)).astype(BF16)
    ng = ng_ref[...]
    hm128 = [lane128 // GLA_DK == h for h in range(GLA_HEADS)]
    hm256 = [lane256 // GLA_DV == h for h in range(GLA_HEADS)]

    E = []
    for i in range(bb):
        z = chunk_rows(z_ref, i)
        E.append(dict(q=z[:, 0:GLA_KW] * (GLA_DK ** -0.5), k=z[:, GLA_KW:2 * GLA_KW],
                      v=z[:, 2 * GLA_KW:2 * GLA_KW + GLA_VW],
                      r=z[:, 2 * GLA_KW + GLA_VW:2 * GLA_KW + 2 * GLA_VW], S=s_s[i]))
    for i, e in enumerate(E):
        e["zz"] = _dot(chunk_rows(zg_ref, i).astype(BF16), wg_ref[...]) + bg_ref[...]
    for e in E:
        loga = _log_sigmoid(e["zz"]) * (1.0 / GLA_TAU)
        if n_valid < L:
            loga = jnp.where(valid, loga, 0.0)
            e["k"] = jnp.where(valid, e["k"], 0.0)
        e["bc"] = _dot_exact_l(tri, loga, 3)
        e["vb"] = e["v"].astype(BF16)
    for i, e in enumerate(E):
        o_s[i] = _dot((e["q"] * jnp.exp(e["bc"])).astype(BF16), e["S"].astype(BF16))
    for j in range(L // SUB - 1):
        r0 = SUB * (j + 1)
        for e in E:
            q, k, v, bc = e["q"], e["k"], e["v"], e["bc"]
            e_j = bc[r0 - 1:r0, :]
            kt = k[r0 - SUB:r0, :] * jnp.exp(e_j - bc[r0 - SUB:r0, :])
            qt = q[r0:, :] * jnp.exp(bc[r0:, :] - e_j)
            kst = jnp.concatenate([jnp.where(hm, kt, 0.0) for hm in hm128], axis=0)
            e["a"] = _dot_nt(qt.astype(BF16), kst.astype(BF16))
        for i, e in enumerate(E):
            vj = e["v"][r0 - SUB:r0, :]
            vst = jnp.concatenate([jnp.where(hm, vj, 0.0) for hm in hm256], axis=0)
            o_s[i, r0:, :] += _dot(e["a"].astype(BF16), vst.astype(BF16))
    for e in E:
        e["o_diag"] = jnp.zeros((L, GLA_VW), F32)
    for d in range(SUB):
        ok = (rowi % SUB) + d < SUB
        for e in E:
            q, k, bc = e["q"], e["k"], e["bc"]
            qd = q if d == 0 else pltpu.roll(q, L - d, 0)
            bcd = bc if d == 0 else pltpu.roll(bc, L - d, 0)
            p = jnp.where(ok, qd * k * jnp.exp(jnp.where(ok, bcd - bc, 0.0)), 0.0)
            e["u"] = _dot(p.astype(BF16), segexp)
        for e in E:
            u = e["u"] * e["v"]
            e["o_diag"] = e["o_diag"] + (u if d == 0 else pltpu.roll(u, d, 0))
    for e in E:
        bc = e["bc"]
        last = bc[L - 1:L, :]
        e["upd"] = _dot_tn((e["k"] * jnp.exp(last - bc)).astype(BF16), e["vb"])
    for i, e in enumerate(E):
        e["o"] = o_s[i] + e["o_diag"]
        e["ms"] = _dot_exact_r(e["o"] * e["o"], segmean, 2)
    for i, e in enumerate(E):
        last_col = e["bc"].T[:, L - 1:L]
        s_s[i] = e["S"] * jnp.exp(last_col) + jnp.where(bd, e["upd"], 0.0)
        r = e["r"]
        y = e["o"] * lax.rsqrt(e["ms"] + NORM_EPS) * ng * (r * _sigmoid(r))
        if single:
            y_ref[i:i + 1, :] = y[0:1, :].astype(y_ref.dtype)
        else:
            y_ref[i] = y.astype(y_ref.dtype)

    @pl.when(c == nc - 1)
    def _():
        for i in range(bb):
            so_ref[i] = _dot_exact_r(s_s[i], gather_m, 3)


def gla(z, zg, wg, bg, state, ng, bb):
    single = z.ndim == 2
    B = z.shape[0]
    nc = 1 if single else z.shape[1] // CHUNK
    zw = 2 * GLA_KW + 2 * GLA_VW
    if single:
        tok = lambda n: pl.BlockSpec((bb, n), lambda b, c: (b, 0))
        y_shape = jax.ShapeDtypeStruct((B, GLA_VW), F32)
    else:
        tok = lambda n: pl.BlockSpec((bb, CHUNK, n), lambda b, c: (b, c, 0))
        y_shape = jax.ShapeDtypeStruct((B, z.shape[1], GLA_VW), BF16)
    s_spec = pl.BlockSpec((bb, GLA_KW, GLA_DV), lambda b, c: (b, 0, 0))
    has_state = state is not None
    return pl.pallas_call(
        functools.partial(_gla_kernel, 1 if single else CHUNK, bb, single, has_state),
        out_shape=[y_shape, jax.ShapeDtypeStruct((B, GLA_KW, GLA_DV), F32)],
        grid=(B // bb, nc),
        in_specs=[tok(zw), tok(128),
                  pl.BlockSpec((128, GLA_KW), lambda b, c: (0, 0)),
                  pl.BlockSpec((1, GLA_KW), lambda b, c: (0, 0))]
                 + ([s_spec] if has_state else []) + [pl.BlockSpec((1, GLA_VW), lambda b, c: (0, 0))],
        out_specs=[tok(GLA_VW), s_spec],
        scratch_shapes=[pltpu.VMEM((bb, GLA_KW, GLA_VW), F32), pltpu.VMEM((bb, CHUNK, GLA_VW), F32)],
        compiler_params=_cp("parallel", "arbitrary"),
        name="gla",
    )(z, zg, wg, bg, *((state,) if has_state else ()), ng)


def _fox_decode_kernel(nps, pt_ref, q_ref, kn_ref, vn_ref, zgn_ref, *refs):
    k_refs = refs[0:nps]
    v_refs = refs[nps:2 * nps]
    lf_refs = refs[2 * nps:3 * nps]
    o_ref, lfo_ref = refs[3 * nps:3 * nps + 2]
    qbd_s, m_s, l_s, acc_s, cc_s = refs[3 * nps + 2:]
    st = pl.program_id(1)
    ns = pl.num_programs(1)
    P = PAGE_SIZE
    eye8 = _iota((8, 128), 0) == _iota((8, 128), 1)
    head8 = _iota((8, FOX_W), 0) == _iota((8, FOX_W), 1) // FOX_DH

    @pl.when(st == 0)
    def _():
        qbd = jnp.where(_iota((16, FOX_W), 0) == _iota((16, FOX_W), 1) // FOX_DH,
                        jnp.broadcast_to(q_ref[...].astype(F32), (16, FOX_W)), 0.0)
        qbd_s[...] = qbd.astype(BF16)
        m_s[...] = jnp.full_like(m_s, NEG_INF)
        l_s[...] = jnp.zeros_like(l_s)
        acc_s[...] = jnp.zeros_like(acc_s)
        cc_s[...] = jnp.zeros_like(cc_s)

    triu = (_iota((P, P), 0) <= _iota((P, P), 1)).astype(BF16)
    qbd = qbd_s[...]
    local = _dot_exact_r(jnp.concatenate([r[...] for r in lf_refs], axis=0), triu, 3)
    carry = cc_s[...]
    s_parts = []
    for i in range(nps):
        loc = local[8 * i:8 * (i + 1), :]
        s_parts.append(_dot(qbd, k_refs[i][...].astype(BF16))[0:8, :] - (loc + carry))
        carry = carry + loc[:, P - 1:P]
    cc_s[...] = carry
    m_loc = s_parts[0]
    for sp in s_parts[1:]:
        m_loc = jnp.maximum(m_loc, sp)
    m = m_s[...]
    m_new = jnp.maximum(m, jnp.max(m_loc, axis=1, keepdims=True))
    a = jnp.exp(m - m_new)
    m_s[...] = m_new
    p_parts = [jnp.exp(sp - m_new) for sp in s_parts]
    p_sum = p_parts[0]
    for pp in p_parts[1:]:
        p_sum = p_sum + pp
    l_s[...] = a * l_s[...] + jnp.sum(p_sum, axis=1, keepdims=True)
    o_even = jnp.zeros((16, FOX_W), F32)
    o_odd = jnp.zeros((16, FOX_W), F32)
    for i in range(0, nps, 2):
        pp = jnp.concatenate([p_parts[i], p_parts[i + 1]], axis=0).astype(BF16)
        o_even = o_even + _dot_nt(pp, v_refs[i][...].astype(BF16))
        o_odd = o_odd + _dot_nt(pp, v_refs[i + 1][...].astype(BF16))
    acc_s[...] = acc_s[...] * a + o_even[0:8, :] + o_odd[8:16, :]

    @pl.when(st == ns - 1)
    def _():
        lfn = _log_sigmoid(zgn_ref[...])
        lfo_ref[...] = lfn
        lfn_col = jnp.sum(jnp.where(eye8, jnp.broadcast_to(lfn, (8, 128)), 0.0), axis=1, keepdims=True)
        q8 = qbd_s[...].astype(F32)[0:8, :]
        s_n = jnp.sum(q8 * kn_ref[...], axis=1, keepdims=True) - (cc_s[...] + lfn_col)
        m = m_s[...]
        m_f = jnp.maximum(m, s_n)
        a = jnp.exp(m - m_f)
        p_n = jnp.exp(s_n - m_f)
        l_f = a * l_s[...] + p_n
        acc = (acc_s[...] * a + p_n * vn_ref[...]) / l_f
        o_ref[...] = jnp.sum(jnp.where(head8, acc, 0.0), axis=0, keepdims=True).astype(o_ref.dtype)


def fox_decode(page_table, q3, kn3, vn3, zgn3, pool_kT, pool_vT, pool_lfT, layer, n_pool, nps):
    DB, n_pages = page_table.shape
    base = layer * n_pool
    row = lambda n: pl.BlockSpec((None, 1, n), lambda b, s, pt: (b, 0, 0))

    def page_spec(i, r):
        return pl.BlockSpec((None, r, PAGE_SIZE), lambda b, s, pt: (base + pt[b, s * nps + i], 0, 0))

    in_specs = ([row(FOX_W), row(FOX_W), row(FOX_W), row(128)]
                + [page_spec(i, FOX_W) for i in range(nps)]
                + [page_spec(i, FOX_W) for i in range(nps)]
                + [page_spec(i, FOX_HEADS) for i in range(nps)])
    grid_spec = pltpu.PrefetchScalarGridSpec(
        num_scalar_prefetch=1,
        grid=(DB, n_pages // nps),
        in_specs=in_specs,
        out_specs=[row(FOX_W), row(128)],
        scratch_shapes=[pltpu.VMEM((16, FOX_W), BF16), pltpu.VMEM((8, 1), F32), pltpu.VMEM((8, 1), F32),
                        pltpu.VMEM((8, FOX_W), F32), pltpu.VMEM((8, 1), F32)])
    return pl.pallas_call(
        functools.partial(_fox_decode_kernel, nps),
        out_shape=[jax.ShapeDtypeStruct((DB, 1, FOX_W), BF16), jax.ShapeDtypeStruct((DB, 1, 128), F32)],
        grid_spec=grid_spec,
        compiler_params=_cp("parallel", "arbitrary"),
        name="fox_decode",
    )(page_table, q3, kn3, vn3, zgn3, *([pool_kT] * nps), *([pool_vT] * nps), *([pool_lfT] * nps))


def _route_rows(lg):
    lane_i = _iota((1, 128), 1)
    is_g = (lane_i >= _R_GRP) & (lane_i < _R_GRP + N_GROUPS)
    is_e = lane_i < N_EXPERTS
    lane = lane_i.astype(F32)
    lane_grp = (lane_i // EXPERTS_PER_GROUP).astype(F32)
    big = 4.0 * 128
    gl = jnp.where(is_g, lg, NEG_INF)
    gmax = jnp.max(gl, axis=1, keepdims=True)
    g_sel = jnp.min(jnp.where(gl == gmax, lane - _R_GRP, big), axis=1, keepdims=True)
    p_grp = 1.0 / jnp.sum(jnp.exp(gl - gmax), axis=1, keepdims=True)
    ev = jnp.where(is_e & (lane_grp == g_sel), lg, NEG_INF)
    v1 = jnp.max(ev, axis=1, keepdims=True)
    i1 = jnp.min(jnp.where(ev == v1, lane, big), axis=1, keepdims=True)
    ev2 = jnp.where(lane == i1, NEG_INF, ev)
    v2 = jnp.max(ev2, axis=1, keepdims=True)
    i2 = jnp.min(jnp.where(ev2 == v2, lane, big), axis=1, keepdims=True)
    t = jnp.exp(v2 - v1)
    w1 = p_grp / (1.0 + t)
    w2 = p_grp * t / (1.0 + t)
    return jnp.where(lane == i1, w1, 0.0) + jnp.where(lane == i2, w2, 0.0)


def _outproj_kernel(yml_ref, ygla_ref, yfox_ref, h_ref, w_ref, g_ref, b_ref, wr_ref, br_ref, h1_ref, comb_ref):
    mix = (_dot(yml_ref[...].astype(BF16), w_ref[0:ML_W, :])
           + _dot(ygla_ref[...].astype(BF16), w_ref[ML_W:ML_W + GLA_VW, :])
           + _dot(yfox_ref[...].astype(BF16), w_ref[ML_W + GLA_VW:, :]))
    h1 = _ln_rows(DN_ALPHA * h_ref[...] + mix, g_ref[...], b_ref[...])
    h1_ref[...] = h1
    comb_ref[...] = _route_rows(_dot(h1.astype(BF16), wr_ref[...]) + br_ref[...])


def outproj(yml, ygla, yfox, h, w, g, b, wr, br, tm):
    T = h.shape[0]
    tok = lambda n: pl.BlockSpec((tm, n), lambda i: (i, 0))
    full = lambda r, n: pl.BlockSpec((r, n), lambda i: (0, 0))
    return pl.pallas_call(
        _outproj_kernel,
        out_shape=[jax.ShapeDtypeStruct((T, D_MODEL), F32), jax.ShapeDtypeStruct((T, 128), F32)],
        grid=(T // tm,),
        in_specs=[tok(ML_W), tok(GLA_VW), tok(FOX_W), tok(D_MODEL), full(D_MODEL, D_MODEL),
                  full(1, D_MODEL), full(1, D_MODEL), full(D_MODEL, 128), full(1, 128)],
        out_specs=[tok(D_MODEL), tok(128)],
        compiler_params=_cp("parallel"),
        name="outproj",
    )(yml, ygla, yfox, h, w, g, b, wr, br)


def _moe_kernel(x_ref, comb_ref, wg_ref, wu_ref, wd_ref, g_ref, b_ref, o_ref, acc_s, xb_s):
    e = pl.program_id(1)

    @pl.when(e == 0)
    def _():
        acc_s[...] = jnp.zeros_like(acc_s)
        xb_s[...] = x_ref[...].astype(BF16)

    xb = xb_s[...]
    a = _dot(xb, wg_ref[...])
    u = _dot(xb, wu_ref[...])
    lane = _iota((1, 128), 1)
    ce = jnp.sum(jnp.where(lane == e, comb_ref[...], 0.0), axis=1, keepdims=True)
    hdn = (a * _sigmoid(a)) * u * ce
    acc_s[...] += _dot(hdn.astype(BF16), wd_ref[...])

    @pl.when(e == pl.num_programs(1) - 1)
    def _():
        o_ref[...] = _ln_rows(DN_ALPHA * x_ref[...] + acc_s[...], g_ref[...], b_ref[...])


def moe(x, comb, wg, wu, wd, g, b, tm):
    T = x.shape[0]
    return pl.pallas_call(
        _moe_kernel,
        out_shape=jax.ShapeDtypeStruct((T, D_MODEL), F32),
        grid=(T // tm, N_EXPERTS),
        in_specs=[pl.BlockSpec((tm, D_MODEL), lambda i, e: (i, 0)),
                  pl.BlockSpec((tm, 128), lambda i, e: (i, 0)),
                  pl.BlockSpec((None, D_MODEL, D_EXPERT), lambda i, e: (e, 0, 0)),
                  pl.BlockSpec((None, D_MODEL, D_EXPERT), lambda i, e: (e, 0, 0)),
                  pl.BlockSpec((None, D_EXPERT, D_MODEL), lambda i, e: (e, 0, 0)),
                  pl.BlockSpec((1, D_MODEL), lambda i, e: (0, 0)),
                  pl.BlockSpec((1, D_MODEL), lambda i, e: (0, 0))],
        out_specs=pl.BlockSpec((tm, D_MODEL), lambda i, e: (i, 0)),
        scratch_shapes=[pltpu.VMEM((tm, D_MODEL), F32), pltpu.VMEM((tm, D_MODEL), BF16)],
        compiler_params=_cp("parallel", "arbitrary"),
        name="moe",
    )(x, comb, wg, wu, wd, g, b)


def _perm_index():
    idx = []
    for name, n in (("ml_q", 256), ("ml_k", 256), ("ml_v", 256), ("ml_o", 256),
                    ("g_q", 128), ("g_k", 128), ("g_v", 256), ("g_r", 256),
                    ("fx_q", 512), ("fx_k", 512), ("fx_v", 512),
                    ("fx_f", 8), ("ml_i", 4), ("ml_f", 4), ("g_lr", 16)):
        idx.extend(range(_OFF[name], _OFF[name] + n))
    return np.asarray(idx, np.int32)


def _col_scale():
    s = np.ones((_NPAD,), np.float32)
    s[_ML0 + ML_W:_ML0 + 2 * ML_W] = ML_DH ** -0.5
    s[_FQ0:_FK0] = FOX_DH ** -0.5
    return s


def _prep_inproj(w_in, b_in):
    idx = _perm_index()
    pad = _NPAD - idx.shape[0]
    scale = jnp.asarray(_col_scale())
    w = jnp.pad(jnp.take(w_in, idx, axis=2), ((0, 0), (0, 0), (0, pad))) * scale
    b = jnp.pad(jnp.take(b_in, idx, axis=1), ((0, 0), (0, pad))) * scale
    return w.astype(BF16), b.reshape(DEPTH, 1, _NPAD)


def _prep_inproj_t(w_in, b_in):
    wt = jnp.transpose(w_in, (0, 2, 1))
    parts, bparts = [], []
    for name, sc in (("fx_q", FOX_DH ** -0.5 * LOG2E), ("fx_k", 1.0), ("fx_v", 1.0)):
        parts.append(wt[:, _OFF[name]:_OFF[name] + FOX_W, :] * sc)
        bparts.append(b_in[:, _OFF[name]:_OFF[name] + FOX_W] * sc)
    return jnp.concatenate(parts, axis=1).astype(BF16), jnp.concatenate(bparts, axis=1)[:, :, None]


def kernel(x_prompt, x_sample, cache_fox_k, cache_fox_v, cache_fox_logf, state_mlstm_C, state_mlstm_n, state_mlstm_m, state_gla_S, page_table, ln_in_g, ln_in_b, w_in, b_in, mlstm_norm_g, gla_w_gate_up, gla_b_gate, gla_norm_g, w_out, ln1_g, ln1_b, w_router_group, b_router_group, w_router_expert, b_router_expert, w_exp_gate, w_exp_up, w_exp_down, ln2_g, ln2_b):
    B, S, D = x_prompt.shape
    DB = x_sample.shape[0]
    T = B * S
    n_pool = cache_fox_k.shape[1]

    w_in_p, b_in_p = _prep_inproj(w_in, b_in)
    w_out_b = w_out.astype(BF16)
    wr = jnp.zeros((DEPTH, D, 128), F32)
    wr = wr.at[:, :, _R_GRP:_R_GRP + N_GROUPS].set(w_router_group).at[:, :, _R_EXP:_R_EXP + N_EXPERTS].set(w_router_expert)
    wr = wr.astype(BF16)
    br = jnp.zeros((DEPTH, 1, 128), F32)
    br = br.at[:, 0, _R_GRP:_R_GRP + N_GROUPS].set(b_router_group).at[:, 0, _R_EXP:_R_EXP + N_EXPERTS].set(b_router_expert)
    wg_b = w_exp_gate.astype(BF16)
    wu_b = w_exp_up.astype(BF16)
    wd_b = w_exp_down.astype(BF16)
    wgl = jnp.zeros((DEPTH, 128, GLA_KW), F32).at[:, _G_LR:_G_LR + GLA_RANK, :].set(gla_w_gate_up).astype(BF16)
    row = lambda a: a.reshape(DEPTH, 1, -1)
    ln1g, ln1b, ln2g, ln2b = row(ln1_g), row(ln1_b), row(ln2_g), row(ln2_b)
    mlng, glng, glbg = row(mlstm_norm_g), row(gla_norm_g), row(gla_b_gate)
    w_in_t, b_in_t = _prep_inproj_t(w_in, b_in)
    pool_k = jnp.transpose(cache_fox_k, (0, 1, 3, 4, 2)).reshape(DEPTH * n_pool, FOX_W, PAGE_SIZE)
    pool_v = jnp.transpose(cache_fox_v, (0, 1, 3, 4, 2)).reshape(DEPTH * n_pool, FOX_W, PAGE_SIZE)
    pool_lf = jnp.transpose(cache_fox_logf, (0, 1, 3, 2)).reshape(DEPTH * n_pool, FOX_HEADS, PAGE_SIZE)

    g_in, b_in_ln = ln_in_g.reshape(1, D), ln_in_b.reshape(1, D)
    hp = ln_rows(x_prompt.reshape(T, D), g_in, b_in_ln, 512)
    hs = ln_rows(x_sample.reshape(DB, D), g_in, b_in_ln, DB)

    outs = {k: [] for k in ("lfp", "ks", "vs", "lfs", "cp", "np", "mp", "cs", "ns", "ms", "gp", "gs")}
    kv_all = None
    tq = min(FOX_TQ, S)
    for l in range(DEPTH):
        zml, zgla, zg, k_rm, qT, kT_all, vT_all = inproj_t(hp, w_in_p[l], b_in_p[l], w_in_t[l], b_in_t[l],
                                                           kv_all, l, B, S, min(INPROJ_TM, S))
        kv_all = (kT_all, vT_all)
        zg3 = zg.reshape(B, S, 128)
        lf, c_sh = foxgate(zg3)
        y_fox = fox_attn(qT, k_rm, vT_all, c_sh, l, B, S, tq)
        y_ml, c_o, n_o, m_o = mlstm(zml.reshape(B, S, 4 * ML_W), zg3, None, mlng[l], min(MLSTM_BB, B))
        y_gla, s_o = gla(zgla.reshape(B, S, -1), zg3, wgl[l], glbg[l], None, glng[l], min(GLA_BB, B))
        h1, comb = outproj(y_ml.reshape(T, ML_W), y_gla.reshape(T, GLA_VW), y_fox, hp, w_out_b[l],
                           ln1g[l], ln1b[l], wr[l], br[l], 512)
        hp = moe(h1, comb, wg_b[l], wu_b[l], wd_b[l], ln2g[l], ln2b[l], min(1024, T))
        outs["lfp"].append(lf)
        outs["cp"].append(c_o); outs["np"].append(n_o); outs["mp"].append(m_o[:, 0, :ML_HEADS]); outs["gp"].append(s_o)

        zml, zgla, fq, fk, fv, zg = inproj(hs, w_in_p[l], b_in_p[l], DB)
        ml_state = (state_mlstm_C[l].astype(F32).reshape(DB, ML_W, ML_DH),
                    state_mlstm_n[l].astype(F32).reshape(DB, 1, ML_W),
                    jnp.pad(state_mlstm_m[l].astype(F32), ((0, 0), (0, 128 - ML_HEADS))).reshape(DB, 1, 128))
        sbb = min(SAMPLE_BB, DB)
        y_ml, c_o, n_o, m_o = mlstm(zml, zg, ml_state, mlng[l], sbb)
        y_gla, s_o = gla(zgla, zg, wgl[l], glbg[l], state_gla_S[l].astype(F32).reshape(DB, GLA_KW, GLA_DV),
                         glng[l], sbb)
        y_fox, lfn = fox_decode(page_table, fq.reshape(DB, 1, FOX_W), fk.reshape(DB, 1, FOX_W),
                                fv.reshape(DB, 1, FOX_W), zg.reshape(DB, 1, 128), pool_k, pool_v, pool_lf,
                                l, n_pool, min(16, page_table.shape[1]))
        h1, comb = outproj(y_ml, y_gla, y_fox.reshape(DB, FOX_W), hs, w_out_b[l],
                           ln1g[l], ln1b[l], wr[l], br[l], DB)
        hs = moe(h1, comb, wg_b[l], wu_b[l], wd_b[l], ln2g[l], ln2b[l], DB)
        outs["ks"].append(fk); outs["vs"].append(fv); outs["lfs"].append(lfn[:, 0, :FOX_HEADS])
        outs["cs"].append(c_o); outs["ns"].append(n_o); outs["ms"].append(m_o[:, 0, :ML_HEADS]); outs["gs"].append(s_o)

    st = lambda k, shape, dt: jnp.stack(outs[k]).reshape((DEPTH,) + shape).astype(dt)
    kd, vd, ld = cache_fox_k.dtype, cache_fox_v.dtype, cache_fox_logf.dtype
    cd, nd, md, sd = state_mlstm_C.dtype, state_mlstm_n.dtype, state_mlstm_m.dtype, state_gla_S.dtype
    kvp = lambda a, dt: jnp.transpose(a.reshape(DEPTH, B, FOX_HEADS, FOX_DH, S), (0, 1, 4, 2, 3)).astype(dt)
    lfp = jnp.transpose(jnp.stack(outs["lfp"]), (0, 1, 3, 2)).astype(ld)
    return (hp.reshape(B, S, D), hs.reshape(DB, 1, D),
            kvp(kv_all[0], kd), kvp(kv_all[1], vd), lfp,
            st("ks", (DB, 1, FOX_HEADS, FOX_DH), kd), st("vs", (DB, 1, FOX_HEADS, FOX_DH), vd),
            st("lfs", (DB, 1, FOX_HEADS), ld),
            st("cp", (B, ML_HEADS, ML_DH, ML_DH), cd), st("np", (B, ML_HEADS, ML_DH), nd), st("mp", (B, ML_HEADS), md),
            st("cs", (DB, ML_HEADS, ML_DH, ML_DH), cd), st("ns", (DB, ML_HEADS, ML_DH), nd), st("ms", (DB, ML_HEADS), md),
            st("gp", (B, GLA_HEADS, GLA_DK, GLA_DV), sd), st("gs", (DB, GLA_HEADS, GLA_DK, GLA_DV), sd))
```

```python
import functools

import numpy as np
import jax
import jax.numpy as jnp
from jax import lax
from jax.experimental import pallas as pl
from jax.experimental.pallas import tpu as pltpu

F32 = jnp.float32
BF16 = jnp.bfloat16
HI = lax.Precision.HIGHEST

D_MODEL = 1024
DEPTH = 4
PAGE_SIZE = 128
ML_DH = 64
ML_HEADS = 4
ML_W = 256
GLA_DK = 32
GLA_DV = 64
GLA_HEADS = 4
GLA_KW = 128
GLA_VW = 256
GLA_RANK = 16
GLA_TAU = 16.0
FOX_DH = 64
FOX_HEADS = 8
FOX_W = 512
CHUNK = 128
SUB = 16
N_GROUPS = 4
EXPERTS_PER_GROUP = 4
N_EXPERTS = 16
D_EXPERT = 256
DN_ALPHA = (2.0 * DEPTH) ** 0.25
LN_EPS = 1e-5
NORM_EPS = 1e-6
NEG_INF = float("-inf")
LOG2E = 1.4426950408889634

_OFF = dict(ml_q=0, ml_k=256, ml_v=512, ml_o=768, ml_i=1024, ml_f=1028, g_q=1032, g_k=1160, g_v=1288,
            g_r=1544, g_lr=1800, fx_q=1816, fx_k=2328, fx_v=2840, fx_f=3352)
_ML0, _GLA0, _FQ0, _FK0, _FV0, _ZG0, _NPAD = 0, 1024, 1792, 2304, 2816, 3328, 3456
_G_FXF, _G_MLI, _G_MLF, _G_LR = 0, 8, 12, 16
_R_EXP, _R_GRP = 0, 16

VMEM_LIMIT = 56 * 1024 * 1024
FOX_TQ = 1024
INPROJ_TM = 512
SAMPLE_BB = 8
DECODE_PAGES = 32
FOX_KS, FOX_QS = 128, 256
FOX_LOOKAHEAD = 8
MLSTM_BB, GLA_BB = 4, 4


def _cp(*sem):
    return pltpu.CompilerParams(dimension_semantics=sem, vmem_limit_bytes=VMEM_LIMIT)


def _dot(a, b, precision=None):
    return jnp.dot(a, b, preferred_element_type=F32, precision=precision)


def _dot_nt(a, b):
    return lax.dot_general(a, b, (((1,), (1,)), ((), ())), preferred_element_type=F32)


def _dot_tn(a, b):
    return lax.dot_general(a, b, (((0,), (0,)), ((), ())), preferred_element_type=F32)


def _split_bf16(x, terms):
    parts, r = [], x
    for _ in range(terms):
        p = r.astype(BF16)
        parts.append(p)
        r = r - p.astype(F32)
    return parts


def _dot_exact_l(sel, x, terms):
    out = None
    for p in _split_bf16(x, terms):
        d = _dot(sel, p)
        out = d if out is None else out + d
    return out


def _dot_exact_r(x, sel, terms):
    out = None
    for p in _split_bf16(x, terms):
        d = _dot(p, sel)
        out = d if out is None else out + d
    return out


def _log_sigmoid(x):
    return jnp.minimum(x, 0.0) - jnp.log1p(jnp.exp(-jnp.abs(x)))


def _sigmoid(x):
    return 1.0 / (1.0 + jnp.exp(-x))


def _ln_rows(x, g, b):
    mu = jnp.mean(x, axis=-1, keepdims=True)
    xc = x - mu
    var = jnp.mean(xc * xc, axis=-1, keepdims=True)
    return xc * lax.rsqrt(var + LN_EPS) * g + b


def _iota(shape, dim):
    return lax.broadcasted_iota(jnp.int32, shape, dim)


def _ln_kernel(x_ref, g_ref, b_ref, o_ref):
    o_ref[...] = _ln_rows(x_ref[...], g_ref[...], b_ref[...])


def ln_rows(x, g, b, tm):
    T, D = x.shape
    return pl.pallas_call(
        _ln_kernel,
        out_shape=jax.ShapeDtypeStruct((T, D), F32),
        grid=(T // tm,),
        in_specs=[pl.BlockSpec((tm, D), lambda i: (i, 0)),
                  pl.BlockSpec((1, D), lambda i: (0, 0)),
                  pl.BlockSpec((1, D), lambda i: (0, 0))],
        out_specs=pl.BlockSpec((tm, D), lambda i: (i, 0)),
        compiler_params=_cp("parallel"),
        name="ln_in",
    )(x, g, b)


_IN_BOUNDS = ((_ML0, _GLA0), (_GLA0, _FQ0), (_FQ0, _FK0), (_FK0, _FV0), (_FV0, _ZG0), (_ZG0, _NPAD))


def _inproj_kernel(h_ref, w_ref, b_ref, *out_refs):
    xb = h_ref[...].astype(BF16)
    for ref, (a, b) in zip(out_refs, _IN_BOUNDS):
        ref[...] = (_dot(xb, w_ref[:, a:b]) + b_ref[:, a:b]).astype(ref.dtype)


def inproj(h, w, b, tm):
    T = h.shape[0]
    widths = [b_ - a_ for a_, b_ in _IN_BOUNDS]
    dtypes = [F32, F32, BF16, F32, F32, F32]
    return pl.pallas_call(
        _inproj_kernel,
        out_shape=[jax.ShapeDtypeStruct((T, n), dt) for n, dt in zip(widths, dtypes)],
        grid=(T // tm,),
        in_specs=[pl.BlockSpec((tm, D_MODEL), lambda i: (i, 0)),
                  pl.BlockSpec((D_MODEL, _NPAD), lambda i: (0, 0)),
                  pl.BlockSpec((1, _NPAD), lambda i: (0, 0))],
        out_specs=[pl.BlockSpec((tm, n), lambda i: (i, 0)) for n in widths],
        compiler_params=_cp("parallel"),
        name="inproj",
    )(h, w, b)


def _inproj_t_kernel(h_ref, w_ref, b_ref, wt_ref, bt_ref, *refs):
    ml_ref, gla_ref, zg_ref, krm_ref, qT_ref, kT_ref, vT_ref = refs[-7:]
    xb = h_ref[...].astype(BF16)
    for ref, (a, b) in ((ml_ref, (_ML0, _GLA0)), (gla_ref, (_GLA0, _FQ0)), (zg_ref, (_ZG0, _NPAD)),
                        (krm_ref, (_FK0, _FV0))):
        ref[...] = (_dot(xb, w_ref[:, a:b]) + b_ref[:, a:b]).astype(ref.dtype)
    for t, ref in enumerate((qT_ref, kT_ref, vT_ref)):
        sl = slice(t * FOX_W, (t + 1) * FOX_W)
        ref[...] = (_dot_nt(wt_ref[sl, :], xb) + bt_ref[sl, :]).astype(ref.dtype)


def inproj_t(h, w, b, wt, bt, kv_all, layer, B, S, tm):
    T = h.shape[0]
    nb = S // tm
    tok = lambda n: pl.BlockSpec((tm, n), lambda i: (i, 0))
    full = lambda r, n: pl.BlockSpec((r, n), lambda i: (0, 0))
    kv_spec = pl.BlockSpec((None, None, FOX_W, tm), lambda i: (layer, i // nb, 0, i % nb))
    kv_shape = jax.ShapeDtypeStruct((DEPTH, B, FOX_W, S), F32)
    in_specs = [tok(D_MODEL), full(D_MODEL, _NPAD), full(1, _NPAD), full(3 * FOX_W, D_MODEL), full(3 * FOX_W, 1)]
    args = [h, w, b, wt, bt]
    aliases = {}
    if kv_all is not None:
        in_specs += [pl.BlockSpec(memory_space=pl.ANY), pl.BlockSpec(memory_space=pl.ANY)]
        args += list(kv_all)
        aliases = {5: 5, 6: 6}
    return pl.pallas_call(
        _inproj_t_kernel,
        out_shape=[jax.ShapeDtypeStruct((T, 4 * ML_W), F32), jax.ShapeDtypeStruct((T, _FQ0 - _GLA0), F32),
                   jax.ShapeDtypeStruct((T, 128), F32), jax.ShapeDtypeStruct((T, FOX_W), BF16),
                   jax.ShapeDtypeStruct((B, FOX_W, S), BF16), kv_shape, kv_shape],
        grid=(T // tm,),
        in_specs=in_specs,
        out_specs=[tok(4 * ML_W), tok(_FQ0 - _GLA0), tok(128), tok(FOX_W),
                   pl.BlockSpec((None, FOX_W, tm), lambda i: (i // nb, 0, i % nb)), kv_spec, kv_spec],
        input_output_aliases=aliases,
        compiler_params=_cp("arbitrary"),
        name="inproj_t",
    )(*args)


def _foxgate_kernel(zg_ref, lft_ref, c_ref):
    S = zg_ref.shape[0]
    tri = (_iota((CHUNK, CHUNK), 0) >= _iota((CHUNK, CHUNK), 1)).astype(BF16)
    src = _iota((128, FOX_W), 0)
    dst = _iota((128, FOX_W), 1)
    place = [jnp.where((dst == 128 * (src // 2) + 3 * (src % 2) + t) & (src < FOX_HEADS), 1.0, 0.0).astype(BF16)
             for t in range(3)]
    carry = jnp.zeros((1, 128), F32)
    for j in range(S // CHUNK):
        sl = slice(j * CHUNK, (j + 1) * CHUNK)
        lf = _log_sigmoid(zg_ref[sl, :])
        lft_ref[:, sl] = lf.T[_G_FXF:_G_FXF + FOX_HEADS, :]
        cs = _dot_exact_l(tri, lf, 3) + carry
        carry = cs[CHUNK - 1:CHUNK, :]
        out = None
        for part, pl_t in zip(_split_bf16(cs * LOG2E, 3), place):
            d = _dot(part, pl_t)
            out = d if out is None else out + d
        c_ref[sl, :] = out.astype(BF16)


def foxgate(zg3):
    B, S, _ = zg3.shape
    return pl.pallas_call(
        _foxgate_kernel,
        out_shape=[jax.ShapeDtypeStruct((B, FOX_HEADS, S), F32), jax.ShapeDtypeStruct((B * S, FOX_W), BF16)],
        grid=(B,),
        in_specs=[pl.BlockSpec((None, S, 128), lambda b: (b, 0, 0))],
        out_specs=[pl.BlockSpec((None, FOX_HEADS, S), lambda b: (b, 0, 0)),
                   pl.BlockSpec((S, FOX_W), lambda b: (b, 0))],
        compiler_params=_cp("parallel"),
        name="foxgate",
    )(zg3)


def _fox_attn_kernel(tq, qT_ref, k_ref, vT_ref, c_ref, o_ref, vb_s):
    qi = pl.program_id(2)

    @pl.when(qi == 0)
    def _():
        vb_s[...] = vT_ref[...].astype(BF16)

    qT = qT_ref[...]
    rowp = _iota((128, 1), 0)
    zero = jnp.zeros_like(qT)
    q_head = (jnp.where(rowp < FOX_DH, qT, zero), jnp.where(rowp >= FOX_DH, qT, zero))
    qTh = tuple(jnp.concatenate(
        [q_head[hh], jnp.broadcast_to(jnp.where(rowp // 3 == hh, -1.0, 0.0), qT.shape).astype(BF16)], axis=0)
        for hh in range(2))
    ks, qs = min(FOX_KS, tq), min(FOX_QS, tq)
    nqs = tq // qs
    tri_mask = _iota((ks, qs), 0) - _iota((ks, qs), 1)

    def block(j, carry, masked):
        off = pl.multiple_of(j * tq, tq)
        carry = list(carry)
        tiles = [(kk, hh, t) for kk in range(tq // ks) for hh in range(2) for t in range(nqs)
                 if not (masked and kk * ks >= (t + 1) * qs)]

        def scores(kk, hh, t):
            ko = pl.multiple_of(off + kk * ks, ks)
            kc = jnp.concatenate([k_ref[pl.ds(ko, ks), :], c_ref[pl.ds(ko, ks), :]], axis=1)
            s = _dot(kc, qTh[hh][:, t * qs:(t + 1) * qs])
            if masked and (kk + 1) * ks - 1 > t * qs:
                s = jnp.where(tri_mask <= t * qs - kk * ks, s, NEG_INF)
            return s

        ready = [scores(*tl) for tl in tiles[:FOX_LOOKAHEAD]]
        for i, (kk, hh, t) in enumerate(tiles):
            if i + FOX_LOOKAHEAD < len(tiles):
                ready.append(scores(*tiles[i + FOX_LOOKAHEAD]))
            s = ready[i]
            ko = pl.multiple_of(off + kk * ks, ks)
            m, l, acc = carry[hh * nqs + t]
            m_new = jnp.maximum(m, jnp.max(s, axis=0, keepdims=True))
            a = jnp.exp2(m - m_new)
            p = jnp.exp2(s - m_new)
            l = a * l + jnp.sum(p, axis=0, keepdims=True)
            vb = vb_s[hh * FOX_DH:(hh + 1) * FOX_DH, pl.ds(ko, ks)]
            acc = a * acc + _dot(vb, p.astype(BF16))
            carry[hh * nqs + t] = (m_new, l, acc)
        return tuple(carry)

    init = tuple((jnp.full((1, qs), NEG_INF, F32), jnp.zeros((1, qs), F32), jnp.zeros((FOX_DH, qs), F32))
                 for _ in range(2 * nqs))
    carry = lax.fori_loop(0, qi, functools.partial(block, masked=False), init)
    carry = block(qi, carry, True)
    oT = jnp.concatenate(
        [jnp.concatenate([carry[hh * nqs + t][2] / carry[hh * nqs + t][1] for t in range(nqs)], axis=1)
         for hh in range(2)], axis=0)
    o_ref[...] = oT.T.astype(o_ref.dtype)


def fox_attn(qT, k_rm, vT_all, c_sh, layer, B, S, tq):
    T = B * S
    nq = S // tq
    return pl.pallas_call(
        functools.partial(_fox_attn_kernel, tq),
        out_shape=jax.ShapeDtypeStruct((T, FOX_W), BF16),
        grid=(B, FOX_HEADS // 2, nq),
        in_specs=[pl.BlockSpec((None, 128, tq), lambda b, hp, qi: (b, hp, qi)),
                  pl.BlockSpec((S, 128), lambda b, hp, qi: (b, hp)),
                  pl.BlockSpec((None, None, 128, S), lambda b, hp, qi: (layer, b, hp, 0)),
                  pl.BlockSpec((S, 128), lambda b, hp, qi: (b, hp))],
        out_specs=pl.BlockSpec((tq, 128), lambda b, hp, qi: (b * nq + qi, hp)),
        scratch_shapes=[pltpu.VMEM((128, S), BF16)],
        compiler_params=_cp("parallel", "arbitrary", "arbitrary"),
        name="fox_attn",
    )(qT, k_rm, vT_all, c_sh)


def _head_tile(dv, heads):
    w = heads * dv
    tile = jnp.where(_iota((dv, w), 1) % dv == _iota((dv, w), 0), 1.0, 0.0).astype(BF16)
    gather = jnp.where(_iota((w, dv), 0) % dv == _iota((w, dv), 1), 1.0, 0.0).astype(BF16)
    return tile, gather


def _mlstm_kernel(n_valid, bb, single, has_state, z_ref, zg_ref, *refs):
    if has_state:
        c0_ref, n0_ref, m0_ref = refs[:3]
        refs = refs[3:]
    ng_ref, y_ref, co_ref, no_ref, mo_ref, c_s, n_s, m_s = refs
    c = pl.program_id(1)
    nc = pl.num_programs(1)
    L = CHUNK
    bd = _iota((ML_W, ML_W), 0) // ML_DH == _iota((ML_W, ML_W), 1) // ML_DH
    tile_m, gather_m = _head_tile(ML_DH, ML_HEADS)

    @pl.when(c == 0)
    def _():
        if has_state:
            for i in range(bb):
                c_s[i] = jnp.where(bd, _dot_exact_r(c0_ref[i], tile_m, 3), 0.0)
            n_s[...] = n0_ref[...]
            m_s[...] = m0_ref[...]
        else:
            c_s[...] = jnp.zeros_like(c_s)
            n_s[...] = jnp.zeros_like(n_s)
            m_s[...] = jnp.zeros_like(m_s)

    row = _iota((L, L), 0)
    col = _iota((L, L), 1)
    causal = row >= col
    tri = causal.astype(BF16)
    lane256 = _iota((1, ML_W), 1)
    lane128 = _iota((1, 128), 1)
    rowi = _iota((L, 1), 0)
    valid = rowi < n_valid

    def chunk_rows(ref, i):
        if single:
            return jnp.where(rowi == 0, jnp.broadcast_to(ref[i:i + 1, :], (L, ref.shape[1])), 0.0)
        return ref[i]

    seg = (_iota((ML_W, 128), 0) // ML_DH == _iota((ML_W, 128), 1)).astype(BF16)
    segmean = (bd.astype(F32) * (1.0 / ML_DH)).astype(BF16)
    ng = ng_ref[...]

    elems = range(bb)
    units = [(i, h) for i in elems for h in range(ML_HEADS)]
    hms = [lane256 // ML_DH == h for h in range(ML_HEADS)]

    E = []
    for i in elems:
        z = chunk_rows(z_ref, i)
        q = z[:, 0:ML_W]
        k = z[:, ML_W:2 * ML_W]
        v = z[:, 2 * ML_W:3 * ML_W]
        g = chunk_rows(zg_ref, i)
        li_all = g
        lf_all = _log_sigmoid(g)
        if n_valid < L:
            li_all = jnp.where(valid, li_all, NEG_INF)
            lf_all = jnp.where(valid, lf_all, 0.0)
        E.append(dict(q=q, k=k, li_all=li_all, lf_all=lf_all, kb=k.astype(BF16), vb=v.astype(BF16),
                      og=z[:, 3 * ML_W:4 * ML_W], C=c_s[i], n_row=n_s[i], m_row=m_s[i]))
    for e in E:
        e["b_all"] = _dot_exact_l(tri, e["lf_all"], 3)
    for e in E:
        e["qC"] = _dot(e["q"].astype(BF16), e["C"].astype(BF16))
    for e in E:
        e["qn"] = _dot_exact_r(e["q"] * e["n_row"], seg, 2)
    for e in E:
        e["liT"] = e["li_all"].T
        e["bT"] = e["b_all"].T

    U = {}
    for (i, h) in units:
        e = E[i]
        U[i, h] = dict(qk=_dot_nt(jnp.where(hms[h], e["q"], 0.0).astype(BF16), e["kb"]))

    for (i, h) in units:
        e, u = E[i], U[i, h]
        b_col = e["b_all"][:, _G_MLF + h:_G_MLF + h + 1]
        b_row = e["bT"][_G_MLF + h:_G_MLF + h + 1, :]
        li_row = e["liT"][_G_MLI + h:_G_MLI + h + 1, :]
        m_prev = e["m_row"][:, h:h + 1]
        dmat = jnp.where(causal, b_col - b_row + li_row, NEG_INF)
        inter = b_col + m_prev
        m_t = jnp.maximum(inter, jnp.max(dmat, axis=1, keepdims=True))
        u.update(b_col=b_col, m_prev=m_prev, m_t=m_t, w_prev=jnp.exp(inter - m_t))
        u["s"] = u["qk"] * jnp.exp(dmat - m_t)

    for (i, h) in units:
        U[i, h]["num"] = _dot(U[i, h]["s"].astype(BF16), E[i]["vb"])

    for e in E:
        e.update(h_acc=jnp.zeros((L, ML_W), F32), kg=jnp.zeros((L, ML_W), F32),
                 gp_row=jnp.zeros((1, ML_W), F32), m_new_row=e["m_row"])
    for (i, h) in units:
        e, u = E[i], U[i, h]
        m_t, w_prev, b_col = u["m_t"], u["w_prev"], u["b_col"]
        den = jnp.sum(u["s"], axis=1, keepdims=True) + w_prev * e["qn"][:, h:h + 1]
        den = jnp.maximum(jnp.abs(den), jnp.exp(-m_t))
        e["h_acc"] = jnp.where(hms[h], (u["num"] + e["qC"] * w_prev) / den, e["h_acc"])
        b_last = b_col[L - 1:L, :]
        m_new = m_t[L - 1:L, :]
        li_col = e["li_all"][:, _G_MLI + h:_G_MLI + h + 1]
        g_rows = jnp.exp(b_last - b_col + li_col - m_new)
        e["kg"] = jnp.where(hms[h], e["k"] * g_rows, e["kg"])
        e["gp_row"] = jnp.where(hms[h], jnp.exp(b_last + u["m_prev"] - m_new), e["gp_row"])
        e["m_new_row"] = jnp.where(lane128 == h, m_new, e["m_new_row"])

    for e in E:
        e["upd"] = _dot_tn(e["kg"].astype(BF16), e["vb"])
    for e in E:
        e["ms"] = _dot_exact_r(e["h_acc"] * e["h_acc"], segmean, 2)
    for i, e in zip(elems, E):
        c_s[i] = e["C"] * e["gp_row"] + jnp.where(bd, e["upd"], 0.0)
        n_s[i] = e["n_row"] * e["gp_row"] + jnp.sum(e["kg"], axis=0, keepdims=True)
        m_s[i] = e["m_new_row"]
        y = e["h_acc"] * lax.rsqrt(e["ms"] + NORM_EPS) * ng * _sigmoid(e["og"])
        if single:
            y_ref[i:i + 1, :] = y[0:1, :].astype(y_ref.dtype)
        else:
            y_ref[i] = y.astype(y_ref.dtype)

    @pl.when(c == nc - 1)
    def _():
        for i in range(bb):
            co_ref[i] = _dot_exact_r(c_s[i], gather_m, 3)
        no_ref[...] = n_s[...]
        mo_ref[...] = m_s[...]


def mlstm(z, zg, state, ng, bb):
    single = z.ndim == 2
    B = z.shape[0]
    nc = 1 if single else z.shape[1] // CHUNK
    st = lambda shape: pl.BlockSpec((bb,) + shape, lambda b, c: (b, 0, 0))
    if single:
        tok = lambda n: pl.BlockSpec((bb, n), lambda b, c: (b, 0))
        y_shape = jax.ShapeDtypeStruct((B, ML_W), F32)
    else:
        tok = lambda n: pl.BlockSpec((bb, CHUNK, n), lambda b, c: (b, c, 0))
        y_shape = jax.ShapeDtypeStruct((B, z.shape[1], ML_W), BF16)
    state_specs = [st((ML_W, ML_DH)), st((1, ML_W)), st((1, 128))]
    has_state = state is not None
    return pl.pallas_call(
        functools.partial(_mlstm_kernel, 1 if single else CHUNK, bb, single, has_state),
        out_shape=[y_shape, jax.ShapeDtypeStruct((B, ML_W, ML_DH), F32),
                   jax.ShapeDtypeStruct((B, 1, ML_W), F32), jax.ShapeDtypeStruct((B, 1, 128), F32)],
        grid=(B // bb, nc),
        in_specs=[tok(4 * ML_W), tok(128)] + (state_specs if has_state else [])
                 + [pl.BlockSpec((1, ML_W), lambda b, c: (0, 0))],
        out_specs=[tok(ML_W)] + state_specs,
        scratch_shapes=[pltpu.VMEM((bb, ML_W, ML_W), F32), pltpu.VMEM((bb, 1, ML_W), F32),
                        pltpu.VMEM((bb, 1, 128), F32)],
        compiler_params=_cp("parallel", "arbitrary"),
        name="mlstm",
    )(z, zg, *(state if has_state else ()), ng)


def _gla_kernel(n_valid, bb, single, has_state, z_ref, zg_ref, wg_ref, bg_ref, *refs):
    if has_state:
        s0_ref = refs[0]
        refs = refs[1:]
    ng_ref, y_ref, so_ref, s_s, o_s = refs
    c = pl.program_id(1)
    nc = pl.num_programs(1)
    L = CHUNK
    bd = _iota((GLA_KW, GLA_VW), 0) // GLA_DK == _iota((GLA_KW, GLA_VW), 1) // GLA_DV
    tile_m, gather_m = _head_tile(GLA_DV, GLA_HEADS)

    @pl.when(c == 0)
    def _():
        if has_state:
            for i in range(bb):
                s_s[i] = jnp.where(bd, _dot_exact_r(s0_ref[i], tile_m, 3), 0.0)
        else:
            s_s[...] = jnp.zeros_like(s_s)

    tri = (_iota((L, L), 0) >= _iota((L, L), 1)).astype(BF16)
    lane128 = _iota((1, GLA_KW), 1)
    lane256 = _iota((1, GLA_VW), 1)
    rowi = _iota((L, 1), 0)
    valid = rowi < n_valid

    def chunk_rows(ref, i):
        if single:
            return jnp.where(rowi == 0, jnp.broadcast_to(ref[i:i + 1, :], (L, ref.shape[1])), 0.0)
        return ref[i]

    segexp = jnp.where(bd, 1.0, 0.0).astype(BF16)
    bdv = _iota((GLA_VW, GLA_VW), 0) // GLA_DV == _iota((GLA_VW, GLA_VW), 1) // GLA_DV
    segmean = (bdv.astype(F32) * (1.0 / GLA_DV)).astype(BF16)
    ng = ng_ref[...]
    hm128 = [lane128 // GLA_DK == h for h in range(GLA_HEADS)]
    hm256 = [lane256 // GLA_DV == h for h in range(GLA_HEADS)]

    E = []
    for i in range(bb):
        z = chunk_rows(z_ref, i)
        E.append(dict(q=z[:, 0:GLA_KW] * (GLA_DK ** -0.5), k=z[:, GLA_KW:2 * GLA_KW],
                      v=z[:, 2 * GLA_KW:2 * GLA_KW + GLA_VW],
                      r=z[:, 2 * GLA_KW + GLA_VW:2 * GLA_KW + 2 * GLA_VW], S=s_s[i]))
    for i, e in enumerate(E):
        e["zz"] = _dot(chunk_rows(zg_ref, i).astype(BF16), wg_ref[...]) + bg_ref[...]
    for e in E:
        loga = _log_sigmoid(e["zz"]) * (1.0 / GLA_TAU)
        if n_valid < L:
            loga = jnp.where(valid, loga, 0.0)
            e["k"] = jnp.where(valid, e["k"], 0.0)
        e["bc"] = _dot_exact_l(tri, loga, 3)
        e["vb"] = e["v"].astype(BF16)
    for i, e in enumerate(E):
        o_s[i] = _dot((e["q"] * jnp.exp(e["bc"])).astype(BF16), e["S"].astype(BF16))
    for j in range(L // SUB - 1):
        r0 = SUB * (j + 1)
        for e in E:
            q, k, v, bc = e["q"], e["k"], e["v"], e["bc"]
            e_j = bc[r0 - 1:r0, :]
            kt = k[r0 - SUB:r0, :] * jnp.exp(e_j - bc[r0 - SUB:r0, :])
            qt = q[r0:, :] * jnp.exp(bc[r0:, :] - e_j)
            kst = jnp.concatenate([jnp.where(hm, kt, 0.0) for hm in hm128], axis=0)
            e["a"] = _dot_nt(qt.astype(BF16), kst.astype(BF16))
        for i, e in enumerate(E):
            vj = e["v"][r0 - SUB:r0, :]
            vst = jnp.concatenate([jnp.where(hm, vj, 0.0) for hm in hm256], axis=0)
            o_s[i, r0:, :] += _dot(e["a"].astype(BF16), vst.astype(BF16))
    for e in E:
        e["o_diag"] = jnp.zeros((L, GLA_VW), F32)
    for d in range(SUB):
        ok = (rowi % SUB) + d < SUB
        for e in E:
            q, k, bc = e["q"], e["k"], e["bc"]
            qd = q if d == 0 else pltpu.roll(q, L - d, 0)
            bcd = bc if d == 0 else pltpu.roll(bc, L - d, 0)
            p = jnp.where(ok, qd * k * jnp.exp(jnp.where(ok, bcd - bc, 0.0)), 0.0)
            e["u"] = _dot(p.astype(BF16), segexp)
        for e in E:
            u = e["u"] * e["v"]
            e["o_diag"] = e["o_diag"] + (u if d == 0 else pltpu.roll(u, d, 0))
    for e in E:
        bc = e["bc"]
        last = bc[L - 1:L, :]
        e["upd"] = _dot_tn((e["k"] * jnp.exp(last - bc)).astype(BF16), e["vb"])
    for i, e in enumerate(E):
        e["o"] = o_s[i] + e["o_diag"]
        e["ms"] = _dot_exact_r(e["o"] * e["o"], segmean, 2)
    for i, e in enumerate(E):
        last_col = e["bc"].T[:, L - 1:L]
        s_s[i] = e["S"] * jnp.exp(last_col) + jnp.where(bd, e["upd"], 0.0)
        r = e["r"]
        y = e["o"] * lax.rsqrt(e["ms"] + NORM_EPS) * ng * (r * _sigmoid(r))
        if single:
            y_ref[i:i + 1, :] = y[0:1, :].astype(y_ref.dtype)
        else:
            y_ref[i] = y.astype(y_ref.dtype)

    @pl.when(c == nc - 1)
    def _():
        for i in range(bb):
            so_ref[i] = _dot_exact_r(s_s[i], gather_m, 3)


def gla(z, zg, wg, bg, state, ng, bb):
    single = z.ndim == 2
    B = z.shape[0]
    nc = 1 if single else z.shape[1] // CHUNK
    zw = 2 * GLA_KW + 2 * GLA_VW
    if single:
        tok = lambda n: pl.BlockSpec((bb, n), lambda b, c: (b, 0))
        y_shape = jax.ShapeDtypeStruct((B, GLA_VW), F32)
    else:
        tok = lambda n: pl.BlockSpec((bb, CHUNK, n), lambda b, c: (b, c, 0))
        y_shape = jax.ShapeDtypeStruct((B, z.shape[1], GLA_VW), BF16)
    s_spec = pl.BlockSpec((bb, GLA_KW, GLA_DV), lambda b, c: (b, 0, 0))
    has_state = state is not None
    return pl.pallas_call(
        functools.partial(_gla_kernel, 1 if single else CHUNK, bb, single, has_state),
        out_shape=[y_shape, jax.ShapeDtypeStruct((B, GLA_KW, GLA_DV), F32)],
        grid=(B // bb, nc),
        in_specs=[tok(zw), tok(128),
                  pl.BlockSpec((128, GLA_KW), lambda b, c: (0, 0)),
                  pl.BlockSpec((1, GLA_KW), lambda b, c: (0, 0))]
                 + ([s_spec] if has_state else []) + [pl.BlockSpec((1, GLA_VW), lambda b, c: (0, 0))],
        out_specs=[tok(GLA_VW), s_spec],
        scratch_shapes=[pltpu.VMEM((bb, GLA_KW, GLA_VW), F32), pltpu.VMEM((bb, CHUNK, GLA_VW), F32)],
        compiler_params=_cp("parallel", "arbitrary"),
        name="gla",
    )(z, zg, wg, bg, *((state,) if has_state else ()), ng)


def _fox_decode_kernel(nps, pt_ref, q_ref, kn_ref, vn_ref, zgn_ref, *refs):
    k_refs = refs[0:nps]
    v_refs = refs[nps:2 * nps]
    lf_refs = refs[2 * nps:3 * nps]
    o_ref, lfo_ref = refs[3 * nps:3 * nps + 2]
    qbd_s, m_s, l_s, acc_s, cc_s = refs[3 * nps + 2:]
    st = pl.program_id(1)
    ns = pl.num_programs(1)
    P = PAGE_SIZE
    eye8 = _iota((8, 128), 0) == _iota((8, 128), 1)
    head8 = _iota((8, FOX_W), 0) == _iota((8, FOX_W), 1) // FOX_DH

    @pl.when(st == 0)
    def _():
        qbd = jnp.where(_iota((16, FOX_W), 0) == _iota((16, FOX_W), 1) // FOX_DH,
                        jnp.broadcast_to(q_ref[...].astype(F32), (16, FOX_W)), 0.0)
        qbd_s[...] = qbd.astype(BF16)
        m_s[...] = jnp.full_like(m_s, NEG_INF)
        l_s[...] = jnp.zeros_like(l_s)
        acc_s[...] = jnp.zeros_like(acc_s)
        cc_s[...] = jnp.zeros_like(cc_s)

    triu = (_iota((P, P), 0) <= _iota((P, P), 1)).astype(BF16)
    qbd = qbd_s[...]
    local = _dot_exact_r(jnp.concatenate([r[...] for r in lf_refs], axis=0), triu, 3)
    carry = cc_s[...]
    s_parts = []
    for i in range(nps):
        loc = local[8 * i:8 * (i + 1), :]
        s_parts.append(_dot(qbd, k_refs[i][...].astype(BF16))[0:8, :] - (loc + carry))
        carry = carry + loc[:, P - 1:P]
    cc_s[...] = carry
    m_loc = s_parts[0]
    for sp in s_parts[1:]:
        m_loc = jnp.maximum(m_loc, sp)
    m = m_s[...]
    m_new = jnp.maximum(m, jnp.max(m_loc, axis=1, keepdims=True))
    a = jnp.exp(m - m_new)
    m_s[...] = m_new
    p_parts = [jnp.exp(sp - m_new) for sp in s_parts]
    p_sum = p_parts[0]
    for pp in p_parts[1:]:
        p_sum = p_sum + pp
    l_s[...] = a * l_s[...] + jnp.sum(p_sum, axis=1, keepdims=True)
    o_even = jnp.zeros((16, FOX_W), F32)
    o_odd = jnp.zeros((16, FOX_W), F32)
    for i in range(0, nps, 2):
        pp = jnp.concatenate([p_parts[i], p_parts[i + 1]], axis=0).astype(BF16)
        o_even = o_even + _dot_nt(pp, v_refs[i][...].astype(BF16))
        o_odd = o_odd + _dot_nt(pp, v_refs[i + 1][...].astype(BF16))
    acc_s[...] = acc_s[...] * a + o_even[0:8, :] + o_odd[8:16, :]

    @pl.when(st == ns - 1)
    def _():
        lfn = _log_sigmoid(zgn_ref[...])
        lfo_ref[...] = lfn
        lfn_col = jnp.sum(jnp.where(eye8, jnp.broadcast_to(lfn, (8, 128)), 0.0), axis=1, keepdims=True)
        q8 = qbd_s[...].astype(F32)[0:8, :]
        s_n = jnp.sum(q8 * kn_ref[...], axis=1, keepdims=True) - (cc_s[...] + lfn_col)
        m = m_s[...]
        m_f = jnp.maximum(m, s_n)
        a = jnp.exp(m - m_f)
        p_n = jnp.exp(s_n - m_f)
        l_f = a * l_s[...] + p_n
        acc = (acc_s[...] * a + p_n * vn_ref[...]) / l_f
        o_ref[...] = jnp.sum(jnp.where(head8, acc, 0.0), axis=0, keepdims=True).astype(o_ref.dtype)


def fox_decode(page_table, q3, kn3, vn3, zgn3, pool_kT, pool_vT, pool_lfT, layer, n_pool, nps):
    DB, n_pages = page_table.shape
    base = layer * n_pool
    row = lambda n: pl.BlockSpec((None, 1, n), lambda b, s, pt: (b, 0, 0))

    def page_spec(i, r):
        return pl.BlockSpec((None, r, PAGE_SIZE), lambda b, s, pt: (base + pt[b, s * nps + i], 0, 0))

    in_specs = ([row(FOX_W), row(FOX_W), row(FOX_W), row(128)]
                + [page_spec(i, FOX_W) for i in range(nps)]
                + [page_spec(i, FOX_W) for i in range(nps)]
                + [page_spec(i, FOX_HEADS) for i in range(nps)])
    grid_spec = pltpu.PrefetchScalarGridSpec(
        num_scalar_prefetch=1,
        grid=(DB, n_pages // nps),
        in_specs=in_specs,
        out_specs=[row(FOX_W), row(128)],
        scratch_shapes=[pltpu.VMEM((16, FOX_W), BF16), pltpu.VMEM((8, 1), F32), pltpu.VMEM((8, 1), F32),
                        pltpu.VMEM((8, FOX_W), F32), pltpu.VMEM((8, 1), F32)])
    return pl.pallas_call(
        functools.partial(_fox_decode_kernel, nps),
        out_shape=[jax.ShapeDtypeStruct((DB, 1, FOX_W), BF16), jax.ShapeDtypeStruct((DB, 1, 128), F32)],
        grid_spec=grid_spec,
        compiler_params=_cp("parallel", "arbitrary"),
        name="fox_decode",
    )(page_table, q3, kn3, vn3, zgn3, *([pool_kT] * nps), *([pool_vT] * nps), *([pool_lfT] * nps))


def _route_rows(lg):
    lane_i = _iota((1, 128), 1)
    is_g = (lane_i >= _R_GRP) & (lane_i < _R_GRP + N_GROUPS)
    is_e = lane_i < N_EXPERTS
    lane = lane_i.astype(F32)
    lane_grp = (lane_i // EXPERTS_PER_GROUP).astype(F32)
    big = 4.0 * 128
    gl = jnp.where(is_g, lg, NEG_INF)
    gmax = jnp.max(gl, axis=1, keepdims=True)
    g_sel = jnp.min(jnp.where(gl == gmax, lane - _R_GRP, big), axis=1, keepdims=True)
    p_grp = 1.0 / jnp.sum(jnp.exp(gl - gmax), axis=1, keepdims=True)
    ev = jnp.where(is_e & (lane_grp == g_sel), lg, NEG_INF)
    v1 = jnp.max(ev, axis=1, keepdims=True)
    i1 = jnp.min(jnp.where(ev == v1, lane, big), axis=1, keepdims=True)
    ev2 = jnp.where(lane == i1, NEG_INF, ev)
    v2 = jnp.max(ev2, axis=1, keepdims=True)
    i2 = jnp.min(jnp.where(ev2 == v2, lane, big), axis=1, keepdims=True)
    t = jnp.exp(v2 - v1)
    w1 = p_grp / (1.0 + t)
    w2 = p_grp * t / (1.0 + t)
    return jnp.where(lane == i1, w1, 0.0) + jnp.where(lane == i2, w2, 0.0)


def _outproj_kernel(yml_ref, ygla_ref, yfox_ref, h_ref, w_ref, g_ref, b_ref, wr_ref, br_ref, h1_ref, comb_ref):
    mix = (_dot(yml_ref[...].astype(BF16), w_ref[0:ML_W, :])
           + _dot(ygla_ref[...].astype(BF16), w_ref[ML_W:ML_W + GLA_VW, :])
           + _dot(yfox_ref[...].astype(BF16), w_ref[ML_W + GLA_VW:, :]))
    h1 = _ln_rows(DN_ALPHA * h_ref[...] + mix, g_ref[...], b_ref[...])
    h1_ref[...] = h1
    comb_ref[...] = _route_rows(_dot(h1.astype(BF16), wr_ref[...]) + br_ref[...])


def outproj(yml, ygla, yfox, h, w, g, b, wr, br, tm):
    T = h.shape[0]
    tok = lambda n: pl.BlockSpec((tm, n), lambda i: (i, 0))
    full = lambda r, n: pl.BlockSpec((r, n), lambda i: (0, 0))
    return pl.pallas_call(
        _outproj_kernel,
        out_shape=[jax.ShapeDtypeStruct((T, D_MODEL), F32), jax.ShapeDtypeStruct((T, 128), F32)],
        grid=(T // tm,),
        in_specs=[tok(ML_W), tok(GLA_VW), tok(FOX_W), tok(D_MODEL), full(D_MODEL, D_MODEL),
                  full(1, D_MODEL), full(1, D_MODEL), full(D_MODEL, 128), full(1, 128)],
        out_specs=[tok(D_MODEL), tok(128)],
        compiler_params=_cp("parallel"),
        name="outproj",
    )(yml, ygla, yfox, h, w, g, b, wr, br)


def _moe_kernel(x_ref, comb_ref, wg_ref, wu_ref, wd_ref, g_ref, b_ref, o_ref, acc_s, xb_s):
    grp = pl.program_id(1)

    @pl.when(grp == 0)
    def _():
        acc_s[...] = jnp.zeros_like(acc_s)
        xb_s[...] = x_ref[...].astype(BF16)

    xb = xb_s[...]
    lane = _iota((1, 128), 1)
    comb = comb_ref[...]
    E = EXPERTS_PER_GROUP

    def up(j):
        return _dot(xb, wg_ref[j]), _dot(xb, wu_ref[j])

    au = [up(0)]
    out = None
    for j in range(E):
        if j + 1 < E:
            au.append(up(j + 1))
        a, u = au[j]
        ce = jnp.sum(jnp.where(lane == grp * E + j, comb, 0.0), axis=1, keepdims=True)
        hdn = (a * _sigmoid(a)) * u * ce
        d = _dot(hdn.astype(BF16), wd_ref[j])
        out = d if out is None else out + d
    acc_s[...] += out

    @pl.when(grp == pl.num_programs(1) - 1)
    def _():
        o_ref[...] = _ln_rows(DN_ALPHA * x_ref[...] + acc_s[...], g_ref[...], b_ref[...])


def moe(x, comb, wg, wu, wd, g, b, tm):
    T = x.shape[0]
    E = EXPERTS_PER_GROUP
    return pl.pallas_call(
        _moe_kernel,
        out_shape=jax.ShapeDtypeStruct((T, D_MODEL), F32),
        grid=(T // tm, N_GROUPS),
        in_specs=[pl.BlockSpec((tm, D_MODEL), lambda i, e: (i, 0)),
                  pl.BlockSpec((tm, 128), lambda i, e: (i, 0)),
                  pl.BlockSpec((E, D_MODEL, D_EXPERT), lambda i, e: (e, 0, 0)),
                  pl.BlockSpec((E, D_MODEL, D_EXPERT), lambda i, e: (e, 0, 0)),
                  pl.BlockSpec((E, D_EXPERT, D_MODEL), lambda i, e: (e, 0, 0)),
                  pl.BlockSpec((1, D_MODEL), lambda i, e: (0, 0)),
                  pl.BlockSpec((1, D_MODEL), lambda i, e: (0, 0))],
        out_specs=pl.BlockSpec((tm, D_MODEL), lambda i, e: (i, 0)),
        scratch_shapes=[pltpu.VMEM((tm, D_MODEL), F32), pltpu.VMEM((tm, D_MODEL), BF16)],
        compiler_params=_cp("parallel", "arbitrary"),
        name="moe",
    )(x, comb, wg, wu, wd, g, b)


def _perm_index():
    idx = []
    for name, n in (("ml_q", 256), ("ml_k", 256), ("ml_v", 256), ("ml_o", 256),
                    ("g_q", 128), ("g_k", 128), ("g_v", 256), ("g_r", 256),
                    ("fx_q", 512), ("fx_k", 512), ("fx_v", 512),
                    ("fx_f", 8), ("ml_i", 4), ("ml_f", 4), ("g_lr", 16)):
        idx.extend(range(_OFF[name], _OFF[name] + n))
    return np.asarray(idx, np.int32)


def _col_scale():
    s = np.ones((_NPAD,), np.float32)
    s[_ML0 + ML_W:_ML0 + 2 * ML_W] = ML_DH ** -0.5
    s[_FQ0:_FK0] = FOX_DH ** -0.5
    return s


def _prep_inproj(w_in, b_in):
    idx = _perm_index()
    pad = _NPAD - idx.shape[0]
    scale = jnp.asarray(_col_scale())
    w = jnp.pad(jnp.take(w_in, idx, axis=2), ((0, 0), (0, 0), (0, pad))) * scale
    b = jnp.pad(jnp.take(b_in, idx, axis=1), ((0, 0), (0, pad))) * scale
    return w.astype(BF16), b.reshape(DEPTH, 1, _NPAD)


def _prep_inproj_t(w_in, b_in):
    wt = jnp.transpose(w_in, (0, 2, 1))
    parts, bparts = [], []
    for name, sc in (("fx_q", FOX_DH ** -0.5 * LOG2E), ("fx_k", 1.0), ("fx_v", 1.0)):
        parts.append(wt[:, _OFF[name]:_OFF[name] + FOX_W, :] * sc)
        bparts.append(b_in[:, _OFF[name]:_OFF[name] + FOX_W] * sc)
    return jnp.concatenate(parts, axis=1).astype(BF16), jnp.concatenate(bparts, axis=1)[:, :, None]


def kernel(x_prompt, x_sample, cache_fox_k, cache_fox_v, cache_fox_logf, state_mlstm_C, state_mlstm_n, state_mlstm_m, state_gla_S, page_table, ln_in_g, ln_in_b, w_in, b_in, mlstm_norm_g, gla_w_gate_up, gla_b_gate, gla_norm_g, w_out, ln1_g, ln1_b, w_router_group, b_router_group, w_router_expert, b_router_expert, w_exp_gate, w_exp_up, w_exp_down, ln2_g, ln2_b):
    B, S, D = x_prompt.shape
    DB = x_sample.shape[0]
    T = B * S
    n_pool = cache_fox_k.shape[1]

    w_in_p, b_in_p = _prep_inproj(w_in, b_in)
    w_out_b = w_out.astype(BF16)
    wr = jnp.zeros((DEPTH, D, 128), F32)
    wr = wr.at[:, :, _R_GRP:_R_GRP + N_GROUPS].set(w_router_group).at[:, :, _R_EXP:_R_EXP + N_EXPERTS].set(w_router_expert)
    wr = wr.astype(BF16)
    br = jnp.zeros((DEPTH, 1, 128), F32)
    br = br.at[:, 0, _R_GRP:_R_GRP + N_GROUPS].set(b_router_group).at[:, 0, _R_EXP:_R_EXP + N_EXPERTS].set(b_router_expert)
    wg_b = w_exp_gate.astype(BF16)
    wu_b = w_exp_up.astype(BF16)
    wd_b = w_exp_down.astype(BF16)
    wgl = jnp.zeros((DEPTH, 128, GLA_KW), F32).at[:, _G_LR:_G_LR + GLA_RANK, :].set(gla_w_gate_up).astype(BF16)
    row = lambda a: a.reshape(DEPTH, 1, -1)
    ln1g, ln1b, ln2g, ln2b = row(ln1_g), row(ln1_b), row(ln2_g), row(ln2_b)
    mlng, glng, glbg = row(mlstm_norm_g), row(gla_norm_g), row(gla_b_gate)
    w_in_t, b_in_t = _prep_inproj_t(w_in, b_in)
    pool_k = jnp.transpose(cache_fox_k, (0, 1, 3, 4, 2)).reshape(DEPTH * n_pool, FOX_W, PAGE_SIZE)
    pool_v = jnp.transpose(cache_fox_v, (0, 1, 3, 4, 2)).reshape(DEPTH * n_pool, FOX_W, PAGE_SIZE)
    pool_lf = jnp.transpose(cache_fox_logf, (0, 1, 3, 2)).reshape(DEPTH * n_pool, FOX_HEADS, PAGE_SIZE)

    g_in, b_in_ln = ln_in_g.reshape(1, D), ln_in_b.reshape(1, D)
    hp = ln_rows(x_prompt.reshape(T, D), g_in, b_in_ln, 512)
    hs = ln_rows(x_sample.reshape(DB, D), g_in, b_in_ln, DB)

    outs = {k: [] for k in ("lfp", "ks", "vs", "lfs", "cp", "np", "mp", "cs", "ns", "ms", "gp", "gs")}
    kv_all = None
    tq = min(FOX_TQ, S)
    for l in range(DEPTH):
        zml, zgla, zg, k_rm, qT, kT_all, vT_all = inproj_t(hp, w_in_p[l], b_in_p[l], w_in_t[l], b_in_t[l],
                                                           kv_all, l, B, S, min(INPROJ_TM, S))
        kv_all = (kT_all, vT_all)
        zg3 = zg.reshape(B, S, 128)
        lf, c_sh = foxgate(zg3)
        y_fox = fox_attn(qT, k_rm, vT_all, c_sh, l, B, S, tq)
        y_ml, c_o, n_o, m_o = mlstm(zml.reshape(B, S, 4 * ML_W), zg3, None, mlng[l], min(MLSTM_BB, B))
        y_gla, s_o = gla(zgla.reshape(B, S, -1), zg3, wgl[l], glbg[l], None, glng[l], min(GLA_BB, B))
        h1, comb = outproj(y_ml.reshape(T, ML_W), y_gla.reshape(T, GLA_VW), y_fox, hp, w_out_b[l],
                           ln1g[l], ln1b[l], wr[l], br[l], 512)
        hp = moe(h1, comb, wg_b[l], wu_b[l], wd_b[l], ln2g[l], ln2b[l], min(1024, T))
        outs["lfp"].append(lf)
        outs["cp"].append(c_o); outs["np"].append(n_o); outs["mp"].append(m_o[:, 0, :ML_HEADS]); outs["gp"].append(s_o)

        zml, zgla, fq, fk, fv, zg = inproj(hs, w_in_p[l], b_in_p[l], DB)
        ml_state = (state_mlstm_C[l].astype(F32).reshape(DB, ML_W, ML_DH),
                    state_mlstm_n[l].astype(F32).reshape(DB, 1, ML_W),
                    jnp.pad(state_mlstm_m[l].astype(F32), ((0, 0), (0, 128 - ML_HEADS))).reshape(DB, 1, 128))
        sbb = min(SAMPLE_BB, DB)
        y_ml, c_o, n_o, m_o = mlstm(zml, zg, ml_state, mlng[l], sbb)
        y_gla, s_o = gla(zgla, zg, wgl[l], glbg[l], state_gla_S[l].astype(F32).reshape(DB, GLA_KW, GLA_DV),
                         glng[l], sbb)
        y_fox, lfn = fox_decode(page_table, fq.reshape(DB, 1, FOX_W), fk.reshape(DB, 1, FOX_W),
                                fv.reshape(DB, 1, FOX_W), zg.reshape(DB, 1, 128), pool_k, pool_v, pool_lf,
                                l, n_pool, min(DECODE_PAGES, page_table.shape[1]))
        h1, comb = outproj(y_ml, y_gla, y_fox.reshape(DB, FOX_W), hs, w_out_b[l],
                           ln1g[l], ln1b[l], wr[l], br[l], DB)
        hs = moe(h1, comb, wg_b[l], wu_b[l], wd_b[l], ln2g[l], ln2b[l], DB)
        outs["ks"].append(fk); outs["vs"].append(fv); outs["lfs"].append(lfn[:, 0, :FOX_HEADS])
        outs["cs"].append(c_o); outs["ns"].append(n_o); outs["ms"].append(m_o[:, 0, :ML_HEADS]); outs["gs"].append(s_o)

    st = lambda k, shape, dt: jnp.stack(outs[k]).reshape((DEPTH,) + shape).astype(dt)
    kd, vd, ld = cache_fox_k.dtype, cache_fox_v.dtype, cache_fox_logf.dtype
    cd, nd, md, sd = state_mlstm_C.dtype, state_mlstm_n.dtype, state_mlstm_m.dtype, state_gla_S.dtype
    kvp = lambda a, dt: jnp.transpose(a.reshape(DEPTH, B, FOX_HEADS, FOX_DH, S), (0, 1, 4, 2, 3)).astype(dt)
    lfp = jnp.transpose(jnp.stack(outs["lfp"]), (0, 1, 3, 2)).astype(ld)
    return (hp.reshape(B, S, D), hs.reshape(DB, 1, D),
            kvp(kv_all[0], kd), kvp(kv_all[1], vd), lfp,
            st("ks", (DB, 1, FOX_HEADS, FOX_DH), kd), st("vs", (DB, 1, FOX_HEADS, FOX_DH), vd),
            st("lfs", (DB, 1, FOX_HEADS), ld),
            st("cp", (B, ML_HEADS, ML_DH, ML_DH), cd), st("np", (B, ML_HEADS, ML_DH), nd), st("mp", (B, ML_HEADS), md),
            st("cs", (DB, ML_HEADS, ML_DH, ML_DH), cd), st("ns", (DB, ML_HEADS, ML_DH), nd), st("ms", (DB, ML_HEADS), md),
            st("gp", (B, GLA_HEADS, GLA_DK, GLA_DV), sd), st("gs", (DB, GLA_HEADS, GLA_DK, GLA_DV), sd))
```

```python
import functools

import numpy as np
import jax
import jax.numpy as jnp
from jax import lax
from jax.experimental import pallas as pl
from jax.experimental.pallas import tpu as pltpu

F32 = jnp.float32
BF16 = jnp.bfloat16
HI = lax.Precision.HIGHEST

D_MODEL = 1024
DEPTH = 4
PAGE_SIZE = 128
ML_DH = 64
ML_HEADS = 4
ML_W = 256
GLA_DK = 32
GLA_DV = 64
GLA_HEADS = 4
GLA_KW = 128
GLA_VW = 256
GLA_RANK = 16
GLA_TAU = 16.0
FOX_DH = 64
FOX_HEADS = 8
FOX_W = 512
CHUNK = 128
SUB = 16
N_GROUPS = 4
EXPERTS_PER_GROUP = 4
N_EXPERTS = 16
D_EXPERT = 256
DN_ALPHA = (2.0 * DEPTH) ** 0.25
LN_EPS = 1e-5
NORM_EPS = 1e-6
NEG_INF = float("-inf")
LOG2E = 1.4426950408889634

_OFF = dict(ml_q=0, ml_k=256, ml_v=512, ml_o=768, ml_i=1024, ml_f=1028, g_q=1032, g_k=1160, g_v=1288,
            g_r=1544, g_lr=1800, fx_q=1816, fx_k=2328, fx_v=2840, fx_f=3352)
_ML0, _GLA0, _FQ0, _FK0, _FV0, _ZG0, _NPAD = 0, 1024, 1792, 2304, 2816, 3328, 3456
_G_FXF, _G_MLI, _G_MLF, _G_LR = 0, 8, 12, 16
_R_EXP, _R_GRP = 0, 16

VMEM_LIMIT = 56 * 1024 * 1024
FOX_TQ = 1024
INPROJ_TM = 512
SAMPLE_BB = 8
DECODE_PAGES = 32
FOX_KS, FOX_QS = 128, 256
FOX_LOOKAHEAD = 8
MLSTM_BB, GLA_BB = 4, 4


def _cp(*sem):
    return pltpu.CompilerParams(dimension_semantics=sem, vmem_limit_bytes=VMEM_LIMIT)


def _dot(a, b, precision=None):
    return jnp.dot(a, b, preferred_element_type=F32, precision=precision)


def _dot_nt(a, b):
    return lax.dot_general(a, b, (((1,), (1,)), ((), ())), preferred_element_type=F32)


def _dot_tn(a, b):
    return lax.dot_general(a, b, (((0,), (0,)), ((), ())), preferred_element_type=F32)


def _split_bf16(x, terms):
    parts, r = [], x
    for _ in range(terms):
        p = r.astype(BF16)
        parts.append(p)
        r = r - p.astype(F32)
    return parts


def _dot_exact_l(sel, x, terms):
    out = None
    for p in _split_bf16(x, terms):
        d = _dot(sel, p)
        out = d if out is None else out + d
    return out


def _dot_exact_r(x, sel, terms):
    out = None
    for p in _split_bf16(x, terms):
        d = _dot(p, sel)
        out = d if out is None else out + d
    return out


def _log_sigmoid(x):
    return jnp.minimum(x, 0.0) - jnp.log1p(jnp.exp(-jnp.abs(x)))


def _sigmoid(x):
    return 1.0 / (1.0 + jnp.exp(-x))


def _ln_rows(x, g, b):
    mu = jnp.mean(x, axis=-1, keepdims=True)
    xc = x - mu
    var = jnp.mean(xc * xc, axis=-1, keepdims=True)
    return xc * lax.rsqrt(var + LN_EPS) * g + b


def _iota(shape, dim):
    return lax.broadcasted_iota(jnp.int32, shape, dim)


def _ln_kernel(x_ref, g_ref, b_ref, o_ref):
    o_ref[...] = _ln_rows(x_ref[...], g_ref[...], b_ref[...])


def ln_rows(x, g, b, tm):
    T, D = x.shape
    return pl.pallas_call(
        _ln_kernel,
        out_shape=jax.ShapeDtypeStruct((T, D), F32),
        grid=(T // tm,),
        in_specs=[pl.BlockSpec((tm, D), lambda i: (i, 0)),
                  pl.BlockSpec((1, D), lambda i: (0, 0)),
                  pl.BlockSpec((1, D), lambda i: (0, 0))],
        out_specs=pl.BlockSpec((tm, D), lambda i: (i, 0)),
        compiler_params=_cp("parallel"),
        name="ln_in",
    )(x, g, b)


_IN_BOUNDS = ((_ML0, _GLA0), (_GLA0, _FQ0), (_FQ0, _FK0), (_FK0, _FV0), (_FV0, _ZG0), (_ZG0, _NPAD))


def _inproj_kernel(h_ref, w_ref, b_ref, *out_refs):
    xb = h_ref[...].astype(BF16)
    for ref, (a, b) in zip(out_refs, _IN_BOUNDS):
        ref[...] = (_dot(xb, w_ref[:, a:b]) + b_ref[:, a:b]).astype(ref.dtype)


def inproj(h, w, layer, b, tm):
    T = h.shape[0]
    widths = [b_ - a_ for a_, b_ in _IN_BOUNDS]
    dtypes = [F32, F32, BF16, F32, F32, F32]
    return pl.pallas_call(
        _inproj_kernel,
        out_shape=[jax.ShapeDtypeStruct((T, n), dt) for n, dt in zip(widths, dtypes)],
        grid=(T // tm,),
        in_specs=[pl.BlockSpec((tm, D_MODEL), lambda i: (i, 0)),
                  pl.BlockSpec((None, D_MODEL, _NPAD), lambda i: (layer, 0, 0)),
                  pl.BlockSpec((1, _NPAD), lambda i: (0, 0))],
        out_specs=[pl.BlockSpec((tm, n), lambda i: (i, 0)) for n in widths],
        compiler_params=_cp("parallel"),
        name="inproj",
    )(h, w, b)


def _inproj_t_kernel(h_ref, w_ref, b_ref, wt_ref, bt_ref, *refs):
    ml_ref, gla_ref, zg_ref, krm_ref, qT_ref, kT_ref, vT_ref = refs[-7:]
    xb = h_ref[...].astype(BF16)
    for ref, (a, b) in ((ml_ref, (_ML0, _GLA0)), (gla_ref, (_GLA0, _FQ0)), (zg_ref, (_ZG0, _NPAD)),
                        (krm_ref, (_FK0, _FV0))):
        ref[...] = (_dot(xb, w_ref[:, a:b]) + b_ref[:, a:b]).astype(ref.dtype)
    for t, ref in enumerate((qT_ref, kT_ref, vT_ref)):
        sl = slice(t * FOX_W, (t + 1) * FOX_W)
        ref[...] = (_dot_nt(wt_ref[sl, :], xb) + bt_ref[sl, :]).astype(ref.dtype)


def inproj_t(h, w, b, wt, bt, kv_all, layer, B, S, tm):
    T = h.shape[0]
    nb = S // tm
    tok = lambda n: pl.BlockSpec((tm, n), lambda i: (i, 0))
    full = lambda r, n: pl.BlockSpec((r, n), lambda i: (0, 0))
    kv_spec = pl.BlockSpec((None, None, FOX_W, tm), lambda i: (layer, i // nb, 0, i % nb))
    kv_shape = jax.ShapeDtypeStruct((DEPTH, B, FOX_W, S), F32)
    lyr = lambda r, n: pl.BlockSpec((None, r, n), lambda i: (layer, 0, 0))
    in_specs = [tok(D_MODEL), lyr(D_MODEL, _NPAD), full(1, _NPAD), lyr(3 * FOX_W, D_MODEL), full(3 * FOX_W, 1)]
    args = [h, w, b, wt, bt]
    aliases = {}
    if kv_all is not None:
        in_specs += [pl.BlockSpec(memory_space=pl.ANY), pl.BlockSpec(memory_space=pl.ANY)]
        args += list(kv_all)
        aliases = {5: 5, 6: 6}
    return pl.pallas_call(
        _inproj_t_kernel,
        out_shape=[jax.ShapeDtypeStruct((T, 4 * ML_W), F32), jax.ShapeDtypeStruct((T, _FQ0 - _GLA0), F32),
                   jax.ShapeDtypeStruct((T, 128), F32), jax.ShapeDtypeStruct((T, FOX_W), BF16),
                   jax.ShapeDtypeStruct((B, FOX_W, S), BF16), kv_shape, kv_shape],
        grid=(T // tm,),
        in_specs=in_specs,
        out_specs=[tok(4 * ML_W), tok(_FQ0 - _GLA0), tok(128), tok(FOX_W),
                   pl.BlockSpec((None, FOX_W, tm), lambda i: (i // nb, 0, i % nb)), kv_spec, kv_spec],
        input_output_aliases=aliases,
        compiler_params=_cp("arbitrary"),
        name="inproj_t",
    )(*args)


def _foxgate_kernel(zg_ref, lft_ref, c_ref):
    S = zg_ref.shape[0]
    tri = (_iota((CHUNK, CHUNK), 0) >= _iota((CHUNK, CHUNK), 1)).astype(BF16)
    src = _iota((128, FOX_W), 0)
    dst = _iota((128, FOX_W), 1)
    place = [jnp.where((dst == 128 * (src // 2) + 3 * (src % 2) + t) & (src < FOX_HEADS), 1.0, 0.0).astype(BF16)
             for t in range(3)]
    carry = jnp.zeros((1, 128), F32)
    for j in range(S // CHUNK):
        sl = slice(j * CHUNK, (j + 1) * CHUNK)
        lf = _log_sigmoid(zg_ref[sl, :])
        lft_ref[:, sl] = lf.T[_G_FXF:_G_FXF + FOX_HEADS, :]
        cs = _dot_exact_l(tri, lf, 3) + carry
        carry = cs[CHUNK - 1:CHUNK, :]
        out = None
        for part, pl_t in zip(_split_bf16(cs * LOG2E, 3), place):
            d = _dot(part, pl_t)
            out = d if out is None else out + d
        c_ref[sl, :] = out.astype(BF16)


def foxgate(zg3):
    B, S, _ = zg3.shape
    return pl.pallas_call(
        _foxgate_kernel,
        out_shape=[jax.ShapeDtypeStruct((B, FOX_HEADS, S), F32), jax.ShapeDtypeStruct((B * S, FOX_W), BF16)],
        grid=(B,),
        in_specs=[pl.BlockSpec((None, S, 128), lambda b: (b, 0, 0))],
        out_specs=[pl.BlockSpec((None, FOX_HEADS, S), lambda b: (b, 0, 0)),
                   pl.BlockSpec((S, FOX_W), lambda b: (b, 0))],
        compiler_params=_cp("parallel"),
        name="foxgate",
    )(zg3)


def _fox_attn_kernel(tq, qT_ref, k_ref, vT_ref, c_ref, o_ref, vb_s):
    qi = pl.program_id(2)

    @pl.when(qi == 0)
    def _():
        vb_s[...] = vT_ref[...].astype(BF16)

    qT = qT_ref[...]
    rowp = _iota((128, 1), 0)
    zero = jnp.zeros_like(qT)
    q_head = (jnp.where(rowp < FOX_DH, qT, zero), jnp.where(rowp >= FOX_DH, qT, zero))
    qTh = tuple(jnp.concatenate(
        [q_head[hh], jnp.broadcast_to(jnp.where(rowp // 3 == hh, -1.0, 0.0), qT.shape).astype(BF16)], axis=0)
        for hh in range(2))
    ks, qs = min(FOX_KS, tq), min(FOX_QS, tq)
    nqs = tq // qs
    tri_mask = _iota((ks, qs), 0) - _iota((ks, qs), 1)

    def block(j, carry, masked):
        off = pl.multiple_of(j * tq, tq)
        carry = list(carry)
        tiles = [(kk, hh, t) for kk in range(tq // ks) for hh in range(2) for t in range(nqs)
                 if not (masked and kk * ks >= (t + 1) * qs)]

        def scores(kk, hh, t):
            ko = pl.multiple_of(off + kk * ks, ks)
            kc = jnp.concatenate([k_ref[pl.ds(ko, ks), :], c_ref[pl.ds(ko, ks), :]], axis=1)
            s = _dot(kc, qTh[hh][:, t * qs:(t + 1) * qs])
            if masked and (kk + 1) * ks - 1 > t * qs:
                s = jnp.where(tri_mask <= t * qs - kk * ks, s, NEG_INF)
            return s

        ready = [scores(*tl) for tl in tiles[:FOX_LOOKAHEAD]]
        for i, (kk, hh, t) in enumerate(tiles):
            if i + FOX_LOOKAHEAD < len(tiles):
                ready.append(scores(*tiles[i + FOX_LOOKAHEAD]))
            s = ready[i]
            ko = pl.multiple_of(off + kk * ks, ks)
            m, l, acc = carry[hh * nqs + t]
            m_new = jnp.maximum(m, jnp.max(s, axis=0, keepdims=True))
            a = jnp.exp2(m - m_new)
            p = jnp.exp2(s - m_new)
            l = a * l + jnp.sum(p, axis=0, keepdims=True)
            vb = vb_s[hh * FOX_DH:(hh + 1) * FOX_DH, pl.ds(ko, ks)]
            acc = a * acc + _dot(vb, p.astype(BF16))
            carry[hh * nqs + t] = (m_new, l, acc)
        return tuple(carry)

    init = tuple((jnp.full((1, qs), NEG_INF, F32), jnp.zeros((1, qs), F32), jnp.zeros((FOX_DH, qs), F32))
                 for _ in range(2 * nqs))
    carry = lax.fori_loop(0, qi, functools.partial(block, masked=False), init)
    carry = block(qi, carry, True)
    oT = jnp.concatenate(
        [jnp.concatenate([carry[hh * nqs + t][2] / carry[hh * nqs + t][1] for t in range(nqs)], axis=1)
         for hh in range(2)], axis=0)
    o_ref[...] = oT.T.astype(o_ref.dtype)


def fox_attn(qT, k_rm, vT_all, c_sh, layer, B, S, tq):
    T = B * S
    nq = S // tq
    return pl.pallas_call(
        functools.partial(_fox_attn_kernel, tq),
        out_shape=jax.ShapeDtypeStruct((T, FOX_W), BF16),
        grid=(B, FOX_HEADS // 2, nq),
        in_specs=[pl.BlockSpec((None, 128, tq), lambda b, hp, qi: (b, hp, qi)),
                  pl.BlockSpec((S, 128), lambda b, hp, qi: (b, hp)),
                  pl.BlockSpec((None, None, 128, S), lambda b, hp, qi: (layer, b, hp, 0)),
                  pl.BlockSpec((S, 128), lambda b, hp, qi: (b, hp))],
        out_specs=pl.BlockSpec((tq, 128), lambda b, hp, qi: (b * nq + qi, hp)),
        scratch_shapes=[pltpu.VMEM((128, S), BF16)],
        compiler_params=_cp("parallel", "arbitrary", "arbitrary"),
        name="fox_attn",
    )(qT, k_rm, vT_all, c_sh)


def _head_tile(dv, heads):
    w = heads * dv
    tile = jnp.where(_iota((dv, w), 1) % dv == _iota((dv, w), 0), 1.0, 0.0).astype(BF16)
    gather = jnp.where(_iota((w, dv), 0) % dv == _iota((w, dv), 1), 1.0, 0.0).astype(BF16)
    return tile, gather


def _mlstm_kernel(n_valid, bb, single, has_state, z_ref, zg_ref, *refs):
    if has_state:
        c0_ref, n0_ref, m0_ref = refs[:3]
        refs = refs[3:]
    ng_ref, y_ref, co_ref, no_ref, mo_ref, c_s, n_s, m_s = refs
    c = pl.program_id(1)
    nc = pl.num_programs(1)
    L = CHUNK
    bd = _iota((ML_W, ML_W), 0) // ML_DH == _iota((ML_W, ML_W), 1) // ML_DH
    tile_m, gather_m = _head_tile(ML_DH, ML_HEADS)

    @pl.when(c == 0)
    def _():
        if has_state:
            for i in range(bb):
                c_s[i] = jnp.where(bd, _dot_exact_r(c0_ref[i], tile_m, 3), 0.0)
            n_s[...] = n0_ref[...]
            m_s[...] = m0_ref[...]
        else:
            c_s[...] = jnp.zeros_like(c_s)
            n_s[...] = jnp.zeros_like(n_s)
            m_s[...] = jnp.zeros_like(m_s)

    row = _iota((L, L), 0)
    col = _iota((L, L), 1)
    causal = row >= col
    tri = causal.astype(BF16)
    lane256 = _iota((1, ML_W), 1)
    lane128 = _iota((1, 128), 1)
    rowi = _iota((L, 1), 0)
    valid = rowi < n_valid

    def chunk_rows(ref, i):
        if single:
            return jnp.where(rowi == 0, jnp.broadcast_to(ref[i:i + 1, :], (L, ref.shape[1])), 0.0)
        return ref[i]

    seg = (_iota((ML_W, 128), 0) // ML_DH == _iota((ML_W, 128), 1)).astype(BF16)
    segmean = (bd.astype(F32) * (1.0 / ML_DH)).astype(BF16)
    ng = ng_ref[...]

    elems = range(bb)
    units = [(i, h) for i in elems for h in range(ML_HEADS)]
    hms = [lane256 // ML_DH == h for h in range(ML_HEADS)]

    E = []
    for i in elems:
        z = chunk_rows(z_ref, i)
        q = z[:, 0:ML_W]
        k = z[:, ML_W:2 * ML_W]
        v = z[:, 2 * ML_W:3 * ML_W]
        g = chunk_rows(zg_ref, i)
        li_all = g
        lf_all = _log_sigmoid(g)
        if n_valid < L:
            li_all = jnp.where(valid, li_all, NEG_INF)
            lf_all = jnp.where(valid, lf_all, 0.0)
        E.append(dict(q=q, k=k, li_all=li_all, lf_all=lf_all, kb=k.astype(BF16), vb=v.astype(BF16),
                      og=z[:, 3 * ML_W:4 * ML_W], C=c_s[i], n_row=n_s[i], m_row=m_s[i]))
    for e in E:
        e["b_all"] = _dot_exact_l(tri, e["lf_all"], 3)
    for e in E:
        e["qC"] = _dot(e["q"].astype(BF16), e["C"].astype(BF16))
    for e in E:
        e["qn"] = _dot_exact_r(e["q"] * e["n_row"], seg, 2)
    for e in E:
        e["liT"] = e["li_all"].T
        e["bT"] = e["b_all"].T

    U = {}
    for (i, h) in units:
        e = E[i]
        U[i, h] = dict(qk=_dot_nt(jnp.where(hms[h], e["q"], 0.0).astype(BF16), e["kb"]))

    for (i, h) in units:
        e, u = E[i], U[i, h]
        b_col = e["b_all"][:, _G_MLF + h:_G_MLF + h + 1]
        b_row = e["bT"][_G_MLF + h:_G_MLF + h + 1, :]
        li_row = e["liT"][_G_MLI + h:_G_MLI + h + 1, :]
        m_prev = e["m_row"][:, h:h + 1]
        dmat = jnp.where(causal, b_col - b_row + li_row, NEG_INF)
        inter = b_col + m_prev
        m_t = jnp.maximum(inter, jnp.max(dmat, axis=1, keepdims=True))
        u.update(b_col=b_col, m_prev=m_prev, m_t=m_t, w_prev=jnp.exp(inter - m_t))
        u["s"] = u["qk"] * jnp.exp(dmat - m_t)

    for (i, h) in units:
        U[i, h]["num"] = _dot(U[i, h]["s"].astype(BF16), E[i]["vb"])

    for e in E:
        e.update(h_acc=jnp.zeros((L, ML_W), F32), kg=jnp.zeros((L, ML_W), F32),
                 gp_row=jnp.zeros((1, ML_W), F32), m_new_row=e["m_row"])
    for (i, h) in units:
        e, u = E[i], U[i, h]
        m_t, w_prev, b_col = u["m_t"], u["w_prev"], u["b_col"]
        den = jnp.sum(u["s"], axis=1, keepdims=True) + w_prev * e["qn"][:, h:h + 1]
        den = jnp.maximum(jnp.abs(den), jnp.exp(-m_t))
        e["h_acc"] = jnp.where(hms[h], (u["num"] + e["qC"] * w_prev) / den, e["h_acc"])
        b_last = b_col[L - 1:L, :]
        m_new = m_t[L - 1:L, :]
        li_col = e["li_all"][:, _G_MLI + h:_G_MLI + h + 1]
        g_rows = jnp.exp(b_last - b_col + li_col - m_new)
        e["kg"] = jnp.where(hms[h], e["k"] * g_rows, e["kg"])
        e["gp_row"] = jnp.where(hms[h], jnp.exp(b_last + u["m_prev"] - m_new), e["gp_row"])
        e["m_new_row"] = jnp.where(lane128 == h, m_new, e["m_new_row"])

    for e in E:
        e["upd"] = _dot_tn(e["kg"].astype(BF16), e["vb"])
    for e in E:
        e["ms"] = _dot_exact_r(e["h_acc"] * e["h_acc"], segmean, 2)
    for i, e in zip(elems, E):
        c_s[i] = e["C"] * e["gp_row"] + jnp.where(bd, e["upd"], 0.0)
        n_s[i] = e["n_row"] * e["gp_row"] + jnp.sum(e["kg"], axis=0, keepdims=True)
        m_s[i] = e["m_new_row"]
        y = e["h_acc"] * lax.rsqrt(e["ms"] + NORM_EPS) * ng * _sigmoid(e["og"])
        if single:
            y_ref[i:i + 1, :] = y[0:1, :].astype(y_ref.dtype)
        else:
            y_ref[i] = y.astype(y_ref.dtype)

    @pl.when(c == nc - 1)
    def _():
        for i in range(bb):
            co_ref[i] = _dot_exact_r(c_s[i], gather_m, 3)
        no_ref[...] = n_s[...]
        mo_ref[...] = m_s[...]


def mlstm(z, zg, state, ng, bb):
    single = z.ndim == 2
    B = z.shape[0]
    nc = 1 if single else z.shape[1] // CHUNK
    st = lambda shape: pl.BlockSpec((bb,) + shape, lambda b, c: (b, 0, 0))
    if single:
        tok = lambda n: pl.BlockSpec((bb, n), lambda b, c: (b, 0))
        y_shape = jax.ShapeDtypeStruct((B, ML_W), F32)
    else:
        tok = lambda n: pl.BlockSpec((bb, CHUNK, n), lambda b, c: (b, c, 0))
        y_shape = jax.ShapeDtypeStruct((B, z.shape[1], ML_W), BF16)
    state_specs = [st((ML_W, ML_DH)), st((1, ML_W)), st((1, 128))]
    has_state = state is not None
    return pl.pallas_call(
        functools.partial(_mlstm_kernel, 1 if single else CHUNK, bb, single, has_state),
        out_shape=[y_shape, jax.ShapeDtypeStruct((B, ML_W, ML_DH), F32),
                   jax.ShapeDtypeStruct((B, 1, ML_W), F32), jax.ShapeDtypeStruct((B, 1, 128), F32)],
        grid=(B // bb, nc),
        in_specs=[tok(4 * ML_W), tok(128)] + (state_specs if has_state else [])
                 + [pl.BlockSpec((1, ML_W), lambda b, c: (0, 0))],
        out_specs=[tok(ML_W)] + state_specs,
        scratch_shapes=[pltpu.VMEM((bb, ML_W, ML_W), F32), pltpu.VMEM((bb, 1, ML_W), F32),
                        pltpu.VMEM((bb, 1, 128), F32)],
        compiler_params=_cp("parallel", "arbitrary"),
        name="mlstm",
    )(z, zg, *(state if has_state else ()), ng)


def _gla_kernel(n_valid, bb, single, has_state, z_ref, zg_ref, wg_ref, bg_ref, *refs):
    if has_state:
        s0_ref = refs[0]
        refs = refs[1:]
    ng_ref, y_ref, so_ref, s_s, o_s = refs
    c = pl.program_id(1)
    nc = pl.num_programs(1)
    L = CHUNK
    bd = _iota((GLA_KW, GLA_VW), 0) // GLA_DK == _iota((GLA_KW, GLA_VW), 1) // GLA_DV
    tile_m, gather_m = _head_tile(GLA_DV, GLA_HEADS)

    @pl.when(c == 0)
    def _():
        if has_state:
            for i in range(bb):
                s_s[i] = jnp.where(bd, _dot_exact_r(s0_ref[i], tile_m, 3), 0.0)
        else:
            s_s[...] = jnp.zeros_like(s_s)

    tri = (_iota((L, L), 0) >= _iota((L, L), 1)).astype(BF16)
    lane128 = _iota((1, GLA_KW), 1)
    lane256 = _iota((1, GLA_VW), 1)
    rowi = _iota((L, 1), 0)
    valid = rowi < n_valid

    def chunk_rows(ref, i):
        if single:
            return jnp.where(rowi == 0, jnp.broadcast_to(ref[i:i + 1, :], (L, ref.shape[1])), 0.0)
        return ref[i]

    segexp = jnp.where(bd, 1.0, 0.0).astype(BF16)
    bdv = _iota((GLA_VW, GLA_VW), 0) // GLA_DV == _iota((GLA_VW, GLA_VW), 1) // GLA_DV
    segmean = (bdv.astype(F32) * (1.0 / GLA_DV)).astype(BF16)
    ng = ng_ref[...]
    hm128 = [lane128 // GLA_DK == h for h in range(GLA_HEADS)]
    hm256 = [lane256 // GLA_DV == h for h in range(GLA_HEADS)]

    E = []
    for i in range(bb):
        z = chunk_rows(z_ref, i)
        E.append(dict(q=z[:, 0:GLA_KW] * (GLA_DK ** -0.5), k=z[:, GLA_KW:2 * GLA_KW],
                      v=z[:, 2 * GLA_KW:2 * GLA_KW + GLA_VW],
                      r=z[:, 2 * GLA_KW + GLA_VW:2 * GLA_KW + 2 * GLA_VW], S=s_s[i]))
    for i, e in enumerate(E):
        e["zz"] = _dot(chunk_rows(zg_ref, i).astype(BF16), wg_ref[...]) + bg_ref[...]
    for e in E:
        loga = _log_sigmoid(e["zz"]) * (1.0 / GLA_TAU)
        if n_valid < L:
            loga = jnp.where(valid, loga, 0.0)
            e["k"] = jnp.where(valid, e["k"], 0.0)
        e["bc"] = _dot_exact_l(tri, loga, 3)
        e["vb"] = e["v"].astype(BF16)
    for i, e in enumerate(E):
        o_s[i] = _dot((e["q"] * jnp.exp(e["bc"])).astype(BF16), e["S"].astype(BF16))
    for j in range(L // SUB - 1):
        r0 = SUB * (j + 1)
        for e in E:
            q, k, v, bc = e["q"], e["k"], e["v"], e["bc"]
            e_j = bc[r0 - 1:r0, :]
            kt = k[r0 - SUB:r0, :] * jnp.exp(e_j - bc[r0 - SUB:r0, :])
            qt = q[r0:, :] * jnp.exp(bc[r0:, :] - e_j)
            kst = jnp.concatenate([jnp.where(hm, kt, 0.0) for hm in hm128], axis=0)
            e["a"] = _dot_nt(qt.astype(BF16), kst.astype(BF16))
        for i, e in enumerate(E):
            vj = e["v"][r0 - SUB:r0, :]
            vst = jnp.concatenate([jnp.where(hm, vj, 0.0) for hm in hm256], axis=0)
            o_s[i, r0:, :] += _dot(e["a"].astype(BF16), vst.astype(BF16))
    for e in E:
        e["o_diag"] = jnp.zeros((L, GLA_VW), F32)
    for d in range(SUB):
        ok = (rowi % SUB) + d < SUB
        for e in E:
            q, k, bc = e["q"], e["k"], e["bc"]
            qd = q if d == 0 else pltpu.roll(q, L - d, 0)
            bcd = bc if d == 0 else pltpu.roll(bc, L - d, 0)
            p = jnp.where(ok, qd * k * jnp.exp(jnp.where(ok, bcd - bc, 0.0)), 0.0)
            e["u"] = _dot(p.astype(BF16), segexp)
        for e in E:
            u = e["u"] * e["v"]
            e["o_diag"] = e["o_diag"] + (u if d == 0 else pltpu.roll(u, d, 0))
    for e in E:
        bc = e["bc"]
        last = bc[L - 1:L, :]
        e["upd"] = _dot_tn((e["k"] * jnp.exp(last - bc)).astype(BF16), e["vb"])
    for i, e in enumerate(E):
        e["o"] = o_s[i] + e["o_diag"]
        e["ms"] = _dot_exact_r(e["o"] * e["o"], segmean, 2)
    for i, e in enumerate(E):
        last_col = e["bc"].T[:, L - 1:L]
        s_s[i] = e["S"] * jnp.exp(last_col) + jnp.where(bd, e["upd"], 0.0)
        r = e["r"]
        y = e["o"] * lax.rsqrt(e["ms"] + NORM_EPS) * ng * (r * _sigmoid(r))
        if single:
            y_ref[i:i + 1, :] = y[0:1, :].astype(y_ref.dtype)
        else:
            y_ref[i] = y.astype(y_ref.dtype)

    @pl.when(c == nc - 1)
    def _():
        for i in range(bb):
            so_ref[i] = _dot_exact_r(s_s[i], gather_m, 3)


def gla(z, zg, wg, bg, state, ng, bb):
    single = z.ndim == 2
    B = z.shape[0]
    nc = 1 if single else z.shape[1] // CHUNK
    zw = 2 * GLA_KW + 2 * GLA_VW
    if single:
        tok = lambda n: pl.BlockSpec((bb, n), lambda b, c: (b, 0))
        y_shape = jax.ShapeDtypeStruct((B, GLA_VW), F32)
    else:
        tok = lambda n: pl.BlockSpec((bb, CHUNK, n), lambda b, c: (b, c, 0))
        y_shape = jax.ShapeDtypeStruct((B, z.shape[1], GLA_VW), BF16)
    s_spec = pl.BlockSpec((bb, GLA_KW, GLA_DV), lambda b, c: (b, 0, 0))
    has_state = state is not None
    return pl.pallas_call(
        functools.partial(_gla_kernel, 1 if single else CHUNK, bb, single, has_state),
        out_shape=[y_shape, jax.ShapeDtypeStruct((B, GLA_KW, GLA_DV), F32)],
        grid=(B // bb, nc),
        in_specs=[tok(zw), tok(128),
                  pl.BlockSpec((128, GLA_KW), lambda b, c: (0, 0)),
                  pl.BlockSpec((1, GLA_KW), lambda b, c: (0, 0))]
                 + ([s_spec] if has_state else []) + [pl.BlockSpec((1, GLA_VW), lambda b, c: (0, 0))],
        out_specs=[tok(GLA_VW), s_spec],
        scratch_shapes=[pltpu.VMEM((bb, GLA_KW, GLA_VW), F32), pltpu.VMEM((bb, CHUNK, GLA_VW), F32)],
        compiler_params=_cp("parallel", "arbitrary"),
        name="gla",
    )(z, zg, wg, bg, *((state,) if has_state else ()), ng)


def _fox_decode_kernel(nps, pt_ref, q_ref, kn_ref, vn_ref, zgn_ref, *refs):
    k_refs = refs[0:nps]
    v_refs = refs[nps:2 * nps]
    lf_refs = refs[2 * nps:3 * nps]
    o_ref, lfo_ref = refs[3 * nps:3 * nps + 2]
    qbd_s, m_s, l_s, acc_s, cc_s = refs[3 * nps + 2:]
    st = pl.program_id(1)
    ns = pl.num_programs(1)
    P = PAGE_SIZE
    eye8 = _iota((8, 128), 0) == _iota((8, 128), 1)
    head8 = _iota((8, FOX_W), 0) == _iota((8, FOX_W), 1) // FOX_DH

    @pl.when(st == 0)
    def _():
        qbd = jnp.where(_iota((16, FOX_W), 0) == _iota((16, FOX_W), 1) // FOX_DH,
                        jnp.broadcast_to(q_ref[...].astype(F32), (16, FOX_W)), 0.0)
        qbd_s[...] = qbd.astype(BF16)
        m_s[...] = jnp.full_like(m_s, NEG_INF)
        l_s[...] = jnp.zeros_like(l_s)
        acc_s[...] = jnp.zeros_like(acc_s)
        cc_s[...] = jnp.zeros_like(cc_s)

    triu = (_iota((P, P), 0) <= _iota((P, P), 1)).astype(BF16)
    qbd = qbd_s[...]
    local = _dot_exact_r(jnp.concatenate([r[...] for r in lf_refs], axis=0), triu, 3)
    carry = cc_s[...]
    s_parts = []
    for i in range(nps):
        loc = local[8 * i:8 * (i + 1), :]
        s_parts.append(_dot(qbd, k_refs[i][...].astype(BF16))[0:8, :] - (loc + carry))
        carry = carry + loc[:, P - 1:P]
    cc_s[...] = carry
    m_loc = s_parts[0]
    for sp in s_parts[1:]:
        m_loc = jnp.maximum(m_loc, sp)
    m = m_s[...]
    m_new = jnp.maximum(m, jnp.max(m_loc, axis=1, keepdims=True))
    a = jnp.exp(m - m_new)
    m_s[...] = m_new
    p_parts = [jnp.exp(sp - m_new) for sp in s_parts]
    p_sum = p_parts[0]
    for pp in p_parts[1:]:
        p_sum = p_sum + pp
    l_s[...] = a * l_s[...] + jnp.sum(p_sum, axis=1, keepdims=True)
    o_even = jnp.zeros((16, FOX_W), F32)
    o_odd = jnp.zeros((16, FOX_W), F32)
    for i in range(0, nps, 2):
        pp = jnp.concatenate([p_parts[i], p_parts[i + 1]], axis=0).astype(BF16)
        o_even = o_even + _dot_nt(pp, v_refs[i][...].astype(BF16))
        o_odd = o_odd + _dot_nt(pp, v_refs[i + 1][...].astype(BF16))
    acc_s[...] = acc_s[...] * a + o_even[0:8, :] + o_odd[8:16, :]

    @pl.when(st == ns - 1)
    def _():
        lfn = _log_sigmoid(zgn_ref[...])
        lfo_ref[...] = lfn
        lfn_col = jnp.sum(jnp.where(eye8, jnp.broadcast_to(lfn, (8, 128)), 0.0), axis=1, keepdims=True)
        q8 = qbd_s[...].astype(F32)[0:8, :]
        s_n = jnp.sum(q8 * kn_ref[...], axis=1, keepdims=True) - (cc_s[...] + lfn_col)
        m = m_s[...]
        m_f = jnp.maximum(m, s_n)
        a = jnp.exp(m - m_f)
        p_n = jnp.exp(s_n - m_f)
        l_f = a * l_s[...] + p_n
        acc = (acc_s[...] * a + p_n * vn_ref[...]) / l_f
        o_ref[...] = jnp.sum(jnp.where(head8, acc, 0.0), axis=0, keepdims=True).astype(o_ref.dtype)


def fox_decode(page_table, q3, kn3, vn3, zgn3, pool_kT, pool_vT, pool_lfT, layer, n_pool, nps):
    DB, n_pages = page_table.shape
    base = layer * n_pool
    row = lambda n: pl.BlockSpec((None, 1, n), lambda b, s, pt: (b, 0, 0))

    def page_spec(i, r):
        return pl.BlockSpec((None, r, PAGE_SIZE), lambda b, s, pt: (base + pt[b, s * nps + i], 0, 0))

    in_specs = ([row(FOX_W), row(FOX_W), row(FOX_W), row(128)]
                + [page_spec(i, FOX_W) for i in range(nps)]
                + [page_spec(i, FOX_W) for i in range(nps)]
                + [page_spec(i, FOX_HEADS) for i in range(nps)])
    grid_spec = pltpu.PrefetchScalarGridSpec(
        num_scalar_prefetch=1,
        grid=(DB, n_pages // nps),
        in_specs=in_specs,
        out_specs=[row(FOX_W), row(128)],
        scratch_shapes=[pltpu.VMEM((16, FOX_W), BF16), pltpu.VMEM((8, 1), F32), pltpu.VMEM((8, 1), F32),
                        pltpu.VMEM((8, FOX_W), F32), pltpu.VMEM((8, 1), F32)])
    return pl.pallas_call(
        functools.partial(_fox_decode_kernel, nps),
        out_shape=[jax.ShapeDtypeStruct((DB, 1, FOX_W), BF16), jax.ShapeDtypeStruct((DB, 1, 128), F32)],
        grid_spec=grid_spec,
        compiler_params=_cp("parallel", "arbitrary"),
        name="fox_decode",
    )(page_table, q3, kn3, vn3, zgn3, *([pool_kT] * nps), *([pool_vT] * nps), *([pool_lfT] * nps))


def _route_rows(lg):
    lane_i = _iota((1, 128), 1)
    is_g = (lane_i >= _R_GRP) & (lane_i < _R_GRP + N_GROUPS)
    is_e = lane_i < N_EXPERTS
    lane = lane_i.astype(F32)
    lane_grp = (lane_i // EXPERTS_PER_GROUP).astype(F32)
    big = 4.0 * 128
    gl = jnp.where(is_g, lg, NEG_INF)
    gmax = jnp.max(gl, axis=1, keepdims=True)
    g_sel = jnp.min(jnp.where(gl == gmax, lane - _R_GRP, big), axis=1, keepdims=True)
    p_grp = 1.0 / jnp.sum(jnp.exp(gl - gmax), axis=1, keepdims=True)
    ev = jnp.where(is_e & (lane_grp == g_sel), lg, NEG_INF)
    v1 = jnp.max(ev, axis=1, keepdims=True)
    i1 = jnp.min(jnp.where(ev == v1, lane, big), axis=1, keepdims=True)
    ev2 = jnp.where(lane == i1, NEG_INF, ev)
    v2 = jnp.max(ev2, axis=1, keepdims=True)
    i2 = jnp.min(jnp.where(ev2 == v2, lane, big), axis=1, keepdims=True)
    t = jnp.exp(v2 - v1)
    w1 = p_grp / (1.0 + t)
    w2 = p_grp * t / (1.0 + t)
    return jnp.where(lane == i1, w1, 0.0) + jnp.where(lane == i2, w2, 0.0)


def _outproj_kernel(yml_ref, ygla_ref, yfox_ref, h_ref, w_ref, g_ref, b_ref, wr_ref, br_ref, h1_ref, comb_ref):
    mix = (_dot(yml_ref[...].astype(BF16), w_ref[0:ML_W, :])
           + _dot(ygla_ref[...].astype(BF16), w_ref[ML_W:ML_W + GLA_VW, :])
           + _dot(yfox_ref[...].astype(BF16), w_ref[ML_W + GLA_VW:, :]))
    h1 = _ln_rows(DN_ALPHA * h_ref[...] + mix, g_ref[...], b_ref[...])
    h1_ref[...] = h1
    comb_ref[...] = _route_rows(_dot(h1.astype(BF16), wr_ref[...]) + br_ref[...])


def outproj(yml, ygla, yfox, h, w, g, b, wr, br, tm):
    T = h.shape[0]
    tok = lambda n: pl.BlockSpec((tm, n), lambda i: (i, 0))
    full = lambda r, n: pl.BlockSpec((r, n), lambda i: (0, 0))
    return pl.pallas_call(
        _outproj_kernel,
        out_shape=[jax.ShapeDtypeStruct((T, D_MODEL), F32), jax.ShapeDtypeStruct((T, 128), F32)],
        grid=(T // tm,),
        in_specs=[tok(ML_W), tok(GLA_VW), tok(FOX_W), tok(D_MODEL), full(D_MODEL, D_MODEL),
                  full(1, D_MODEL), full(1, D_MODEL), full(D_MODEL, 128), full(1, 128)],
        out_specs=[tok(D_MODEL), tok(128)],
        compiler_params=_cp("parallel"),
        name="outproj",
    )(yml, ygla, yfox, h, w, g, b, wr, br)


def _moe_kernel(x_ref, comb_ref, wg_ref, wu_ref, wd_ref, g_ref, b_ref, o_ref, acc_s, xb_s):
    grp = pl.program_id(1)

    @pl.when(grp == 0)
    def _():
        acc_s[...] = jnp.zeros_like(acc_s)
        xb_s[...] = x_ref[...].astype(BF16)

    xb = xb_s[...]
    lane = _iota((1, 128), 1)
    comb = comb_ref[...]
    E = EXPERTS_PER_GROUP

    def up(j):
        return _dot(xb, wg_ref[j]), _dot(xb, wu_ref[j])

    au = [up(0)]
    out = None
    for j in range(E):
        if j + 1 < E:
            au.append(up(j + 1))
        a, u = au[j]
        ce = jnp.sum(jnp.where(lane == grp * E + j, comb, 0.0), axis=1, keepdims=True)
        hdn = (a * _sigmoid(a)) * u * ce
        d = _dot(hdn.astype(BF16), wd_ref[j])
        out = d if out is None else out + d
    acc_s[...] += out

    @pl.when(grp == pl.num_programs(1) - 1)
    def _():
        o_ref[...] = _ln_rows(DN_ALPHA * x_ref[...] + acc_s[...], g_ref[...], b_ref[...])


def moe(x, comb, wg, wu, wd, layer, g, b, tm):
    T = x.shape[0]
    E = EXPERTS_PER_GROUP
    return pl.pallas_call(
        _moe_kernel,
        out_shape=jax.ShapeDtypeStruct((T, D_MODEL), F32),
        grid=(T // tm, N_GROUPS),
        in_specs=[pl.BlockSpec((tm, D_MODEL), lambda i, e: (i, 0)),
                  pl.BlockSpec((tm, 128), lambda i, e: (i, 0)),
                  pl.BlockSpec((None, E, D_MODEL, D_EXPERT), lambda i, e: (layer, e, 0, 0)),
                  pl.BlockSpec((None, E, D_MODEL, D_EXPERT), lambda i, e: (layer, e, 0, 0)),
                  pl.BlockSpec((None, E, D_EXPERT, D_MODEL), lambda i, e: (layer, e, 0, 0)),
                  pl.BlockSpec((1, D_MODEL), lambda i, e: (0, 0)),
                  pl.BlockSpec((1, D_MODEL), lambda i, e: (0, 0))],
        out_specs=pl.BlockSpec((tm, D_MODEL), lambda i, e: (i, 0)),
        scratch_shapes=[pltpu.VMEM((tm, D_MODEL), F32), pltpu.VMEM((tm, D_MODEL), BF16)],
        compiler_params=_cp("parallel", "arbitrary"),
        name="moe",
    )(x, comb, wg, wu, wd, g, b)


def _perm_index():
    idx = []
    for name, n in (("ml_q", 256), ("ml_k", 256), ("ml_v", 256), ("ml_o", 256),
                    ("g_q", 128), ("g_k", 128), ("g_v", 256), ("g_r", 256),
                    ("fx_q", 512), ("fx_k", 512), ("fx_v", 512),
                    ("fx_f", 8), ("ml_i", 4), ("ml_f", 4), ("g_lr", 16)):
        idx.extend(range(_OFF[name], _OFF[name] + n))
    return np.asarray(idx, np.int32)


def _col_scale():
    s = np.ones((_NPAD,), np.float32)
    s[_ML0 + ML_W:_ML0 + 2 * ML_W] = ML_DH ** -0.5
    s[_FQ0:_FK0] = FOX_DH ** -0.5
    return s


def _prep_inproj(w_in, b_in):
    idx = _perm_index()
    pad = _NPAD - idx.shape[0]
    scale = jnp.asarray(_col_scale())
    w = jnp.pad(jnp.take(w_in, idx, axis=2), ((0, 0), (0, 0), (0, pad))) * scale
    b = jnp.pad(jnp.take(b_in, idx, axis=1), ((0, 0), (0, pad))) * scale
    return w.astype(BF16), b.reshape(DEPTH, 1, _NPAD)


def _prep_inproj_t(w_in, b_in):
    wt = jnp.transpose(w_in, (0, 2, 1))
    parts, bparts = [], []
    for name, sc in (("fx_q", FOX_DH ** -0.5 * LOG2E), ("fx_k", 1.0), ("fx_v", 1.0)):
        parts.append(wt[:, _OFF[name]:_OFF[name] + FOX_W, :] * sc)
        bparts.append(b_in[:, _OFF[name]:_OFF[name] + FOX_W] * sc)
    return jnp.concatenate(parts, axis=1).astype(BF16), jnp.concatenate(bparts, axis=1)[:, :, None]


def kernel(x_prompt, x_sample, cache_fox_k, cache_fox_v, cache_fox_logf, state_mlstm_C, state_mlstm_n, state_mlstm_m, state_gla_S, page_table, ln_in_g, ln_in_b, w_in, b_in, mlstm_norm_g, gla_w_gate_up, gla_b_gate, gla_norm_g, w_out, ln1_g, ln1_b, w_router_group, b_router_group, w_router_expert, b_router_expert, w_exp_gate, w_exp_up, w_exp_down, ln2_g, ln2_b):
    B, S, D = x_prompt.shape
    DB = x_sample.shape[0]
    T = B * S
    n_pool = cache_fox_k.shape[1]

    w_in_p, b_in_p = _prep_inproj(w_in, b_in)
    w_out_b = w_out.astype(BF16)
    wr = jnp.zeros((DEPTH, D, 128), F32)
    wr = wr.at[:, :, _R_GRP:_R_GRP + N_GROUPS].set(w_router_group).at[:, :, _R_EXP:_R_EXP + N_EXPERTS].set(w_router_expert)
    wr = wr.astype(BF16)
    br = jnp.zeros((DEPTH, 1, 128), F32)
    br = br.at[:, 0, _R_GRP:_R_GRP + N_GROUPS].set(b_router_group).at[:, 0, _R_EXP:_R_EXP + N_EXPERTS].set(b_router_expert)
    wg_b = w_exp_gate.astype(BF16)
    wu_b = w_exp_up.astype(BF16)
    wd_b = w_exp_down.astype(BF16)
    wgl = jnp.zeros((DEPTH, 128, GLA_KW), F32).at[:, _G_LR:_G_LR + GLA_RANK, :].set(gla_w_gate_up).astype(BF16)
    row = lambda a: a.reshape(DEPTH, 1, -1)
    ln1g, ln1b, ln2g, ln2b = row(ln1_g), row(ln1_b), row(ln2_g), row(ln2_b)
    mlng, glng, glbg = row(mlstm_norm_g), row(gla_norm_g), row(gla_b_gate)
    w_in_t, b_in_t = _prep_inproj_t(w_in, b_in)
    pool_k = jnp.transpose(cache_fox_k, (0, 1, 3, 4, 2)).reshape(DEPTH * n_pool, FOX_W, PAGE_SIZE)
    pool_v = jnp.transpose(cache_fox_v, (0, 1, 3, 4, 2)).reshape(DEPTH * n_pool, FOX_W, PAGE_SIZE)
    pool_lf = jnp.transpose(cache_fox_logf, (0, 1, 3, 2)).reshape(DEPTH * n_pool, FOX_HEADS, PAGE_SIZE)

    g_in, b_in_ln = ln_in_g.reshape(1, D), ln_in_b.reshape(1, D)
    hp = ln_rows(x_prompt.reshape(T, D), g_in, b_in_ln, 512)
    hs = ln_rows(x_sample.reshape(DB, D), g_in, b_in_ln, DB)

    outs = {k: [] for k in ("lfp", "ks", "vs", "lfs", "cp", "np", "mp", "cs", "ns", "ms", "gp", "gs")}
    kv_all = None
    tq = min(FOX_TQ, S)
    for l in range(DEPTH):
        zml, zgla, zg, k_rm, qT, kT_all, vT_all = inproj_t(hp, w_in_p, b_in_p[l], w_in_t, b_in_t[l],
                                                           kv_all, l, B, S, min(INPROJ_TM, S))
        kv_all = (kT_all, vT_all)
        zg3 = zg.reshape(B, S, 128)
        lf, c_sh = foxgate(zg3)
        y_fox = fox_attn(qT, k_rm, vT_all, c_sh, l, B, S, tq)
        y_ml, c_o, n_o, m_o = mlstm(zml.reshape(B, S, 4 * ML_W), zg3, None, mlng[l], min(MLSTM_BB, B))
        y_gla, s_o = gla(zgla.reshape(B, S, -1), zg3, wgl[l], glbg[l], None, glng[l], min(GLA_BB, B))
        h1, comb = outproj(y_ml.reshape(T, ML_W), y_gla.reshape(T, GLA_VW), y_fox, hp, w_out_b[l],
                           ln1g[l], ln1b[l], wr[l], br[l], 512)
        hp = moe(h1, comb, wg_b, wu_b, wd_b, l, ln2g[l], ln2b[l], min(1024, T))
        outs["lfp"].append(lf)
        outs["cp"].append(c_o); outs["np"].append(n_o); outs["mp"].append(m_o[:, 0, :ML_HEADS]); outs["gp"].append(s_o)

        zml, zgla, fq, fk, fv, zg = inproj(hs, w_in_p, l, b_in_p[l], DB)
        ml_state = (state_mlstm_C[l].astype(F32).reshape(DB, ML_W, ML_DH),
                    state_mlstm_n[l].astype(F32).reshape(DB, 1, ML_W),
                    jnp.pad(state_mlstm_m[l].astype(F32), ((0, 0), (0, 128 - ML_HEADS))).reshape(DB, 1, 128))
        sbb = min(SAMPLE_BB, DB)
        y_ml, c_o, n_o, m_o = mlstm(zml, zg, ml_state, mlng[l], sbb)
        y_gla, s_o = gla(zgla, zg, wgl[l], glbg[l], state_gla_S[l].astype(F32).reshape(DB, GLA_KW, GLA_DV),
                         glng[l], sbb)
        y_fox, lfn = fox_decode(page_table, fq.reshape(DB, 1, FOX_W), fk.reshape(DB, 1, FOX_W),
                                fv.reshape(DB, 1, FOX_W), zg.reshape(DB, 1, 128), pool_k, pool_v, pool_lf,
                                l, n_pool, min(DECODE_PAGES, page_table.shape[1]))
        h1, comb = outproj(y_ml, y_gla, y_fox.reshape(DB, FOX_W), hs, w_out_b[l],
                           ln1g[l], ln1b[l], wr[l], br[l], DB)
        hs = moe(h1, comb, wg_b, wu_b, wd_b, l, ln2g[l], ln2b[l], DB)
        outs["ks"].append(fk); outs["vs"].append(fv); outs["lfs"].append(lfn[:, 0, :FOX_HEADS])
        outs["cs"].append(c_o); outs["ns"].append(n_o); outs["ms"].append(m_o[:, 0, :ML_HEADS]); outs["gs"].append(s_o)

    st = lambda k, shape, dt: jnp.stack(outs[k]).reshape((DEPTH,) + shape).astype(dt)
    kd, vd, ld = cache_fox_k.dtype, cache_fox_v.dtype, cache_fox_logf.dtype
    cd, nd, md, sd = state_mlstm_C.dtype, state_mlstm_n.dtype, state_mlstm_m.dtype, state_gla_S.dtype
    kvp = lambda a, dt: jnp.transpose(a.reshape(DEPTH, B, FOX_HEADS, FOX_DH, S), (0, 1, 4, 2, 3)).astype(dt)
    lfp = jnp.transpose(jnp.stack(outs["lfp"]), (0, 1, 3, 2)).astype(ld)
    return (hp.reshape(B, S, D), hs.reshape(DB, 1, D),
            kvp(kv_all[0], kd), kvp(kv_all[1], vd), lfp,
            st("ks", (DB, 1, FOX_HEADS, FOX_DH), kd), st("vs", (DB, 1, FOX_HEADS, FOX_DH), vd),
            st("lfs", (DB, 1, FOX_HEADS), ld),
            st("cp", (B, ML_HEADS, ML_DH, ML_DH), cd), st("np", (B, ML_HEADS, ML_DH), nd), st("mp", (B, ML_HEADS), md),
            st("cs", (DB, ML_HEADS, ML_DH, ML_DH), cd), st("ns", (DB, ML_HEADS, ML_DH), nd), st("ms", (DB, ML_HEADS), md),
            st("gp", (B, GLA_HEADS, GLA_DK, GLA_DV), sd), st("gs", (DB, GLA_HEADS, GLA_DK, GLA_DV), sd))
```

```python
import functools

import numpy as np
import jax
import jax.numpy as jnp
from jax import lax
from jax.experimental import pallas as pl
from jax.experimental.pallas import tpu as pltpu

F32 = jnp.float32
BF16 = jnp.bfloat16
HI = lax.Precision.HIGHEST

D_MODEL = 1024
DEPTH = 4
PAGE_SIZE = 128
ML_DH = 64
ML_HEADS = 4
ML_W = 256
GLA_DK = 32
GLA_DV = 64
GLA_HEADS = 4
GLA_KW = 128
GLA_VW = 256
GLA_RANK = 16
GLA_TAU = 16.0
FOX_DH = 64
FOX_HEADS = 8
FOX_W = 512
CHUNK = 128
SUB = 16
N_GROUPS = 4
EXPERTS_PER_GROUP = 4
N_EXPERTS = 16
D_EXPERT = 256
DN_ALPHA = (2.0 * DEPTH) ** 0.25
LN_EPS = 1e-5
NORM_EPS = 1e-6
NEG_INF = float("-inf")
LOG2E = 1.4426950408889634

_OFF = dict(ml_q=0, ml_k=256, ml_v=512, ml_o=768, ml_i=1024, ml_f=1028, g_q=1032, g_k=1160, g_v=1288,
            g_r=1544, g_lr=1800, fx_q=1816, fx_k=2328, fx_v=2840, fx_f=3352)
_ML0, _GLA0, _FQ0, _FK0, _FV0, _ZG0, _NPAD = 0, 1024, 1792, 2304, 2816, 3328, 3456
_G_FXF, _G_MLI, _G_MLF, _G_LR = 0, 8, 12, 16
_R_EXP, _R_GRP = 0, 16

VMEM_LIMIT = 58 * 1024 * 1024
FOX_TQ = 1024
INPROJ_TM = 512
SAMPLE_BB = 8
DECODE_PAGES = 32
FOX_KS, FOX_QS = 128, 256
FOX_LOOKAHEAD = 8
MLSTM_BB, GLA_BB = 4, 4


def _cp(*sem):
    return pltpu.CompilerParams(dimension_semantics=sem, vmem_limit_bytes=VMEM_LIMIT)


def _dot(a, b, precision=None):
    return jnp.dot(a, b, preferred_element_type=F32, precision=precision)


def _dot_nt(a, b):
    return lax.dot_general(a, b, (((1,), (1,)), ((), ())), preferred_element_type=F32)


def _dot_tn(a, b):
    return lax.dot_general(a, b, (((0,), (0,)), ((), ())), preferred_element_type=F32)


def _split_bf16(x, terms):
    parts, r = [], x
    for _ in range(terms):
        p = r.astype(BF16)
        parts.append(p)
        r = r - p.astype(F32)
    return parts


def _dot_exact_l(sel, x, terms):
    out = None
    for p in _split_bf16(x, terms):
        d = _dot(sel, p)
        out = d if out is None else out + d
    return out


def _dot_exact_r(x, sel, terms):
    out = None
    for p in _split_bf16(x, terms):
        d = _dot(p, sel)
        out = d if out is None else out + d
    return out


def _log_sigmoid(x):
    return jnp.minimum(x, 0.0) - jnp.log1p(jnp.exp(-jnp.abs(x)))


def _sigmoid(x):
    return 1.0 / (1.0 + jnp.exp(-x))


def _ln_rows(x, g, b):
    mu = jnp.mean(x, axis=-1, keepdims=True)
    xc = x - mu
    var = jnp.mean(xc * xc, axis=-1, keepdims=True)
    return xc * lax.rsqrt(var + LN_EPS) * g + b


def _iota(shape, dim):
    return lax.broadcasted_iota(jnp.int32, shape, dim)


def _ln_kernel(x_ref, g_ref, b_ref, o_ref):
    o_ref[...] = _ln_rows(x_ref[...], g_ref[...], b_ref[...])


def ln_rows(x, g, b, tm):
    T, D = x.shape
    return pl.pallas_call(
        _ln_kernel,
        out_shape=jax.ShapeDtypeStruct((T, D), F32),
        grid=(T // tm,),
        in_specs=[pl.BlockSpec((tm, D), lambda i: (i, 0)),
                  pl.BlockSpec((1, D), lambda i: (0, 0)),
                  pl.BlockSpec((1, D), lambda i: (0, 0))],
        out_specs=pl.BlockSpec((tm, D), lambda i: (i, 0)),
        compiler_params=_cp("parallel"),
        name="ln_in",
    )(x, g, b)


_IN_BOUNDS = ((_ML0, _GLA0), (_GLA0, _FQ0), (_FQ0, _FK0), (_FK0, _FV0), (_FV0, _ZG0), (_ZG0, _NPAD))


def _inproj_kernel(h_ref, w_ref, b_ref, *out_refs):
    xb = h_ref[...].astype(BF16)
    for ref, (a, b) in zip(out_refs, _IN_BOUNDS):
        ref[...] = (_dot(xb, w_ref[:, a:b]) + b_ref[:, a:b]).astype(ref.dtype)


def inproj(h, w, layer, b, tm):
    T = h.shape[0]
    widths = [b_ - a_ for a_, b_ in _IN_BOUNDS]
    dtypes = [F32, F32, BF16, F32, F32, F32]
    return pl.pallas_call(
        _inproj_kernel,
        out_shape=[jax.ShapeDtypeStruct((T, n), dt) for n, dt in zip(widths, dtypes)],
        grid=(T // tm,),
        in_specs=[pl.BlockSpec((tm, D_MODEL), lambda i: (i, 0)),
                  pl.BlockSpec((None, D_MODEL, _NPAD), lambda i: (layer, 0, 0)),
                  pl.BlockSpec((1, _NPAD), lambda i: (0, 0))],
        out_specs=[pl.BlockSpec((tm, n), lambda i: (i, 0)) for n in widths],
        compiler_params=_cp("parallel"),
        name="inproj",
    )(h, w, b)


def _inproj_t_kernel(h_ref, w_ref, b_ref, wt_ref, bt_ref, *refs):
    ml_ref, gla_ref, zg_ref, krm_ref, qT_ref, kT_ref, vT_ref = refs[-7:]
    xb = h_ref[...].astype(BF16)
    for ref, (a, b) in ((ml_ref, (_ML0, _GLA0)), (gla_ref, (_GLA0, _FQ0)), (zg_ref, (_ZG0, _NPAD)),
                        (krm_ref, (_FK0, _FV0))):
        ref[...] = (_dot(xb, w_ref[:, a:b]) + b_ref[:, a:b]).astype(ref.dtype)
    for t, ref in enumerate((qT_ref, kT_ref, vT_ref)):
        sl = slice(t * FOX_W, (t + 1) * FOX_W)
        ref[...] = (_dot_nt(wt_ref[sl, :], xb) + bt_ref[sl, :]).astype(ref.dtype)


def inproj_t(h, w, b, wt, bt, kv_all, layer, B, S, tm):
    T = h.shape[0]
    nb = S // tm
    tok = lambda n: pl.BlockSpec((tm, n), lambda i: (i, 0))
    full = lambda r, n: pl.BlockSpec((r, n), lambda i: (0, 0))
    kv_spec = pl.BlockSpec((None, None, FOX_W, tm), lambda i: (layer, i // nb, 0, i % nb))
    kv_shape = jax.ShapeDtypeStruct((DEPTH, B, FOX_W, S), F32)
    lyr = lambda r, n: pl.BlockSpec((None, r, n), lambda i: (layer, 0, 0))
    in_specs = [tok(D_MODEL), lyr(D_MODEL, _NPAD), full(1, _NPAD), lyr(3 * FOX_W, D_MODEL), full(3 * FOX_W, 1)]
    args = [h, w, b, wt, bt]
    aliases = {}
    if kv_all is not None:
        in_specs += [pl.BlockSpec(memory_space=pl.ANY), pl.BlockSpec(memory_space=pl.ANY)]
        args += list(kv_all)
        aliases = {5: 5, 6: 6}
    return pl.pallas_call(
        _inproj_t_kernel,
        out_shape=[jax.ShapeDtypeStruct((T, 4 * ML_W), F32), jax.ShapeDtypeStruct((T, _FQ0 - _GLA0), F32),
                   jax.ShapeDtypeStruct((T, 128), F32), jax.ShapeDtypeStruct((T, FOX_W), BF16),
                   jax.ShapeDtypeStruct((B, FOX_W, S), BF16), kv_shape, kv_shape],
        grid=(T // tm,),
        in_specs=in_specs,
        out_specs=[tok(4 * ML_W), tok(_FQ0 - _GLA0), tok(128), tok(FOX_W),
                   pl.BlockSpec((None, FOX_W, tm), lambda i: (i // nb, 0, i % nb)), kv_spec, kv_spec],
        input_output_aliases=aliases,
        compiler_params=_cp("arbitrary"),
        name="inproj_t",
    )(*args)


def _foxgate_kernel(zg_ref, lft_ref, c_ref):
    S = zg_ref.shape[0]
    tri = (_iota((CHUNK, CHUNK), 0) >= _iota((CHUNK, CHUNK), 1)).astype(BF16)
    src = _iota((128, FOX_W), 0)
    dst = _iota((128, FOX_W), 1)
    place = [jnp.where((dst == 128 * (src // 2) + 3 * (src % 2) + t) & (src < FOX_HEADS), 1.0, 0.0).astype(BF16)
             for t in range(3)]
    carry = jnp.zeros((1, 128), F32)
    for j in range(S // CHUNK):
        sl = slice(j * CHUNK, (j + 1) * CHUNK)
        lf = _log_sigmoid(zg_ref[sl, :])
        lft_ref[:, sl] = lf.T[_G_FXF:_G_FXF + FOX_HEADS, :]
        cs = _dot_exact_l(tri, lf, 3) + carry
        carry = cs[CHUNK - 1:CHUNK, :]
        out = None
        for part, pl_t in zip(_split_bf16(cs * LOG2E, 3), place):
            d = _dot(part, pl_t)
            out = d if out is None else out + d
        c_ref[sl, :] = out.astype(BF16)


def foxgate(zg3):
    B, S, _ = zg3.shape
    return pl.pallas_call(
        _foxgate_kernel,
        out_shape=[jax.ShapeDtypeStruct((B, FOX_HEADS, S), F32), jax.ShapeDtypeStruct((B * S, FOX_W), BF16)],
        grid=(B,),
        in_specs=[pl.BlockSpec((None, S, 128), lambda b: (b, 0, 0))],
        out_specs=[pl.BlockSpec((None, FOX_HEADS, S), lambda b: (b, 0, 0)),
                   pl.BlockSpec((S, FOX_W), lambda b: (b, 0))],
        compiler_params=_cp("parallel"),
        name="foxgate",
    )(zg3)


def _fox_attn_kernel(tq, qT_ref, k_ref, vT_ref, c_ref, o_ref, vb_s):
    qi = pl.program_id(2)

    @pl.when(qi == 0)
    def _():
        vb_s[...] = vT_ref[...].astype(BF16)

    qT = qT_ref[...]
    rowp = _iota((128, 1), 0)
    zero = jnp.zeros_like(qT)
    q_head = (jnp.where(rowp < FOX_DH, qT, zero), jnp.where(rowp >= FOX_DH, qT, zero))
    qTh = tuple(jnp.concatenate(
        [q_head[hh], jnp.broadcast_to(jnp.where(rowp // 3 == hh, -1.0, 0.0), qT.shape).astype(BF16)], axis=0)
        for hh in range(2))
    ks, qs = min(FOX_KS, tq), min(FOX_QS, tq)
    nqs = tq // qs
    tri_mask = _iota((ks, qs), 0) - _iota((ks, qs), 1)

    def block(j, carry, masked):
        off = pl.multiple_of(j * tq, tq)
        carry = list(carry)
        tiles = [(kk, hh, t) for kk in range(tq // ks) for hh in range(2) for t in range(nqs)
                 if not (masked and kk * ks >= (t + 1) * qs)]

        def scores(kk, hh, t):
            ko = pl.multiple_of(off + kk * ks, ks)
            kc = jnp.concatenate([k_ref[pl.ds(ko, ks), :], c_ref[pl.ds(ko, ks), :]], axis=1)
            s = _dot(kc, qTh[hh][:, t * qs:(t + 1) * qs])
            if masked and (kk + 1) * ks - 1 > t * qs:
                s = jnp.where(tri_mask <= t * qs - kk * ks, s, NEG_INF)
            return s

        ready = [scores(*tl) for tl in tiles[:FOX_LOOKAHEAD]]
        for i, (kk, hh, t) in enumerate(tiles):
            if i + FOX_LOOKAHEAD < len(tiles):
                ready.append(scores(*tiles[i + FOX_LOOKAHEAD]))
            s = ready[i]
            ko = pl.multiple_of(off + kk * ks, ks)
            m, l, acc = carry[hh * nqs + t]
            m_new = jnp.maximum(m, jnp.max(s, axis=0, keepdims=True))
            a = jnp.exp2(m - m_new)
            p = jnp.exp2(s - m_new)
            l = a * l + jnp.sum(p, axis=0, keepdims=True)
            vb = vb_s[hh * FOX_DH:(hh + 1) * FOX_DH, pl.ds(ko, ks)]
            acc = a * acc + _dot(vb, p.astype(BF16))
            carry[hh * nqs + t] = (m_new, l, acc)
        return tuple(carry)

    init = tuple((jnp.full((1, qs), NEG_INF, F32), jnp.zeros((1, qs), F32), jnp.zeros((FOX_DH, qs), F32))
                 for _ in range(2 * nqs))
    carry = lax.fori_loop(0, qi, functools.partial(block, masked=False), init)
    carry = block(qi, carry, True)
    oT = jnp.concatenate(
        [jnp.concatenate([carry[hh * nqs + t][2] / carry[hh * nqs + t][1] for t in range(nqs)], axis=1)
         for hh in range(2)], axis=0)
    o_ref[...] = oT.T.astype(o_ref.dtype)


def fox_attn(qT, k_rm, vT_all, c_sh, layer, B, S, tq):
    T = B * S
    nq = S // tq
    return pl.pallas_call(
        functools.partial(_fox_attn_kernel, tq),
        out_shape=jax.ShapeDtypeStruct((T, FOX_W), BF16),
        grid=(B, FOX_HEADS // 2, nq),
        in_specs=[pl.BlockSpec((None, 128, tq), lambda b, hp, qi: (b, hp, qi)),
                  pl.BlockSpec((S, 128), lambda b, hp, qi: (b, hp)),
                  pl.BlockSpec((None, None, 128, S), lambda b, hp, qi: (layer, b, hp, 0)),
                  pl.BlockSpec((S, 128), lambda b, hp, qi: (b, hp))],
        out_specs=pl.BlockSpec((tq, 128), lambda b, hp, qi: (b * nq + qi, hp)),
        scratch_shapes=[pltpu.VMEM((128, S), BF16)],
        compiler_params=_cp("parallel", "arbitrary", "arbitrary"),
        name="fox_attn",
    )(qT, k_rm, vT_all, c_sh)


def _head_tile(dv, heads):
    w = heads * dv
    tile = jnp.where(_iota((dv, w), 1) % dv == _iota((dv, w), 0), 1.0, 0.0).astype(BF16)
    gather = jnp.where(_iota((w, dv), 0) % dv == _iota((w, dv), 1), 1.0, 0.0).astype(BF16)
    return tile, gather


def _mlstm_kernel(n_valid, bb, single, has_state, z_ref, zg_ref, *refs):
    if has_state:
        c0_ref, n0_ref, m0_ref = refs[:3]
        refs = refs[3:]
    ng_ref, y_ref, co_ref, no_ref, mo_ref, c_s, n_s, m_s = refs
    c = pl.program_id(1)
    nc = pl.num_programs(1)
    L = CHUNK
    bd = _iota((ML_W, ML_W), 0) // ML_DH == _iota((ML_W, ML_W), 1) // ML_DH
    tile_m, gather_m = _head_tile(ML_DH, ML_HEADS)

    @pl.when(c == 0)
    def _():
        if has_state:
            for i in range(bb):
                c_s[i] = jnp.where(bd, _dot_exact_r(c0_ref[i], tile_m, 3), 0.0)
            n_s[...] = n0_ref[...]
            m_s[...] = m0_ref[...]
        else:
            c_s[...] = jnp.zeros_like(c_s)
            n_s[...] = jnp.zeros_like(n_s)
            m_s[...] = jnp.zeros_like(m_s)

    row = _iota((L, L), 0)
    col = _iota((L, L), 1)
    causal = row >= col
    tri = causal.astype(BF16)
    lane256 = _iota((1, ML_W), 1)
    lane128 = _iota((1, 128), 1)
    rowi = _iota((L, 1), 0)
    valid = rowi < n_valid

    def chunk_rows(ref, i):
        if single:
            return jnp.where(rowi == 0, jnp.broadcast_to(ref[i:i + 1, :], (L, ref.shape[1])), 0.0)
        return ref[i]

    seg = (_iota((ML_W, 128), 0) // ML_DH == _iota((ML_W, 128), 1)).astype(BF16)
    segmean = (bd.astype(F32) * (1.0 / ML_DH)).astype(BF16)
    ng = ng_ref[...]

    elems = range(bb)
    units = [(i, h) for i in elems for h in range(ML_HEADS)]
    hms = [lane256 // ML_DH == h for h in range(ML_HEADS)]

    E = []
    for i in elems:
        z = chunk_rows(z_ref, i)
        q = z[:, 0:ML_W]
        k = z[:, ML_W:2 * ML_W]
        v = z[:, 2 * ML_W:3 * ML_W]
        g = chunk_rows(zg_ref, i)
        li_all = g
        lf_all = _log_sigmoid(g)
        if n_valid < L:
            li_all = jnp.where(valid, li_all, NEG_INF)
            lf_all = jnp.where(valid, lf_all, 0.0)
        E.append(dict(q=q, k=k, li_all=li_all, lf_all=lf_all, kb=k.astype(BF16), vb=v.astype(BF16),
                      og=z[:, 3 * ML_W:4 * ML_W], C=c_s[i], n_row=n_s[i], m_row=m_s[i]))
    for e in E:
        e["b_all"] = _dot_exact_l(tri, e["lf_all"], 3)
    for e in E:
        e["qC"] = _dot(e["q"].astype(BF16), e["C"].astype(BF16))
    for e in E:
        e["qn"] = _dot_exact_r(e["q"] * e["n_row"], seg, 2)
    for e in E:
        e["liT"] = e["li_all"].T
        e["bT"] = e["b_all"].T

    U = {}
    for (i, h) in units:
        e = E[i]
        U[i, h] = dict(qk=_dot_nt(jnp.where(hms[h], e["q"], 0.0).astype(BF16), e["kb"]))

    for (i, h) in units:
        e, u = E[i], U[i, h]
        b_col = e["b_all"][:, _G_MLF + h:_G_MLF + h + 1]
        b_row = e["bT"][_G_MLF + h:_G_MLF + h + 1, :]
        li_row = e["liT"][_G_MLI + h:_G_MLI + h + 1, :]
        m_prev = e["m_row"][:, h:h + 1]
        dmat = jnp.where(causal, b_col - b_row + li_row, NEG_INF)
        inter = b_col + m_prev
        m_t = jnp.maximum(inter, jnp.max(dmat, axis=1, keepdims=True))
        u.update(b_col=b_col, m_prev=m_prev, m_t=m_t, w_prev=jnp.exp(inter - m_t))
        u["s"] = u["qk"] * jnp.exp(dmat - m_t)

    for (i, h) in units:
        U[i, h]["num"] = _dot(U[i, h]["s"].astype(BF16), E[i]["vb"])

    for e in E:
        e.update(h_acc=jnp.zeros((L, ML_W), F32), kg=jnp.zeros((L, ML_W), F32),
                 gp_row=jnp.zeros((1, ML_W), F32), m_new_row=e["m_row"])
    for (i, h) in units:
        e, u = E[i], U[i, h]
        m_t, w_prev, b_col = u["m_t"], u["w_prev"], u["b_col"]
        den = jnp.sum(u["s"], axis=1, keepdims=True) + w_prev * e["qn"][:, h:h + 1]
        den = jnp.maximum(jnp.abs(den), jnp.exp(-m_t))
        e["h_acc"] = jnp.where(hms[h], (u["num"] + e["qC"] * w_prev) / den, e["h_acc"])
        b_last = b_col[L - 1:L, :]
        m_new = m_t[L - 1:L, :]
        li_col = e["li_all"][:, _G_MLI + h:_G_MLI + h + 1]
        g_rows = jnp.exp(b_last - b_col + li_col - m_new)
        e["kg"] = jnp.where(hms[h], e["k"] * g_rows, e["kg"])
        e["gp_row"] = jnp.where(hms[h], jnp.exp(b_last + u["m_prev"] - m_new), e["gp_row"])
        e["m_new_row"] = jnp.where(lane128 == h, m_new, e["m_new_row"])

    for e in E:
        e["upd"] = _dot_tn(e["kg"].astype(BF16), e["vb"])
    for e in E:
        e["ms"] = _dot_exact_r(e["h_acc"] * e["h_acc"], segmean, 2)
    for i, e in zip(elems, E):
        c_s[i] = e["C"] * e["gp_row"] + jnp.where(bd, e["upd"], 0.0)
        n_s[i] = e["n_row"] * e["gp_row"] + jnp.sum(e["kg"], axis=0, keepdims=True)
        m_s[i] = e["m_new_row"]
        y = e["h_acc"] * lax.rsqrt(e["ms"] + NORM_EPS) * ng * _sigmoid(e["og"])
        if single:
            y_ref[i:i + 1, :] = y[0:1, :].astype(y_ref.dtype)
        else:
            y_ref[i] = y.astype(y_ref.dtype)

    @pl.when(c == nc - 1)
    def _():
        for i in range(bb):
            co_ref[i] = _dot_exact_r(c_s[i], gather_m, 3)
        no_ref[...] = n_s[...]
        mo_ref[...] = m_s[...]


def mlstm(z, zg, state, ng, bb):
    single = z.ndim == 2
    B = z.shape[0]
    nc = 1 if single else z.shape[1] // CHUNK
    st = lambda shape: pl.BlockSpec((bb,) + shape, lambda b, c: (b, 0, 0))
    if single:
        tok = lambda n: pl.BlockSpec((bb, n), lambda b, c: (b, 0))
        y_shape = jax.ShapeDtypeStruct((B, ML_W), F32)
    else:
        tok = lambda n: pl.BlockSpec((bb, CHUNK, n), lambda b, c: (b, c, 0))
        y_shape = jax.ShapeDtypeStruct((B, z.shape[1], ML_W), BF16)
    state_specs = [st((ML_W, ML_DH)), st((1, ML_W)), st((1, 128))]
    has_state = state is not None
    return pl.pallas_call(
        functools.partial(_mlstm_kernel, 1 if single else CHUNK, bb, single, has_state),
        out_shape=[y_shape, jax.ShapeDtypeStruct((B, ML_W, ML_DH), F32),
                   jax.ShapeDtypeStruct((B, 1, ML_W), F32), jax.ShapeDtypeStruct((B, 1, 128), F32)],
        grid=(B // bb, nc),
        in_specs=[tok(4 * ML_W), tok(128)] + (state_specs if has_state else [])
                 + [pl.BlockSpec((1, ML_W), lambda b, c: (0, 0))],
        out_specs=[tok(ML_W)] + state_specs,
        scratch_shapes=[pltpu.VMEM((bb, ML_W, ML_W), F32), pltpu.VMEM((bb, 1, ML_W), F32),
                        pltpu.VMEM((bb, 1, 128), F32)],
        compiler_params=_cp("parallel", "arbitrary"),
        name="mlstm",
    )(z, zg, *(state if has_state else ()), ng)


def _gla_kernel(n_valid, bb, single, has_state, z_ref, zg_ref, wg_ref, bg_ref, *refs):
    if has_state:
        s0_ref = refs[0]
        refs = refs[1:]
    ng_ref, y_ref, so_ref, s_s, o_s = refs
    c = pl.program_id(1)
    nc = pl.num_programs(1)
    L = CHUNK
    bd = _iota((GLA_KW, GLA_VW), 0) // GLA_DK == _iota((GLA_KW, GLA_VW), 1) // GLA_DV
    tile_m, gather_m = _head_tile(GLA_DV, GLA_HEADS)

    @pl.when(c == 0)
    def _():
        if has_state:
            for i in range(bb):
                s_s[i] = jnp.where(bd, _dot_exact_r(s0_ref[i], tile_m, 3), 0.0)
        else:
            s_s[...] = jnp.zeros_like(s_s)

    tri = (_iota((L, L), 0) >= _iota((L, L), 1)).astype(BF16)
    lane128 = _iota((1, GLA_KW), 1)
    lane256 = _iota((1, GLA_VW), 1)
    rowi = _iota((L, 1), 0)
    valid = rowi < n_valid

    def chunk_rows(ref, i):
        if single:
            return jnp.where(rowi == 0, jnp.broadcast_to(ref[i:i + 1, :], (L, ref.shape[1])), 0.0)
        return ref[i]

    segexp = jnp.where(bd, 1.0, 0.0).astype(BF16)
    bdv = _iota((GLA_VW, GLA_VW), 0) // GLA_DV == _iota((GLA_VW, GLA_VW), 1) // GLA_DV
    segmean = (bdv.astype(F32) * (1.0 / GLA_DV)).astype(BF16)
    ng = ng_ref[...]
    hm128 = [lane128 // GLA_DK == h for h in range(GLA_HEADS)]
    hm256 = [lane256 // GLA_DV == h for h in range(GLA_HEADS)]

    E = []
    for i in range(bb):
        z = chunk_rows(z_ref, i)
        E.append(dict(q=z[:, 0:GLA_KW] * (GLA_DK ** -0.5), k=z[:, GLA_KW:2 * GLA_KW],
                      v=z[:, 2 * GLA_KW:2 * GLA_KW + GLA_VW],
                      r=z[:, 2 * GLA_KW + GLA_VW:2 * GLA_KW + 2 * GLA_VW], S=s_s[i]))
    for i, e in enumerate(E):
        e["zz"] = _dot(chunk_rows(zg_ref, i).astype(BF16), wg_ref[...]) + bg_ref[...]
    for e in E:
        loga = _log_sigmoid(e["zz"]) * (1.0 / GLA_TAU)
        if n_valid < L:
            loga = jnp.where(valid, loga, 0.0)
            e["k"] = jnp.where(valid, e["k"], 0.0)
        e["bc"] = _dot_exact_l(tri, loga, 3)
        e["vb"] = e["v"].astype(BF16)
    for i, e in enumerate(E):
        o_s[i] = _dot((e["q"] * jnp.exp(e["bc"])).astype(BF16), e["S"].astype(BF16))
    for j in range(L // SUB - 1):
        r0 = SUB * (j + 1)
        for e in E:
            q, k, v, bc = e["q"], e["k"], e["v"], e["bc"]
            e_j = bc[r0 - 1:r0, :]
            kt = k[r0 - SUB:r0, :] * jnp.exp(e_j - bc[r0 - SUB:r0, :])
            qt = q[r0:, :] * jnp.exp(bc[r0:, :] - e_j)
            kst = jnp.concatenate([jnp.where(hm, kt, 0.0) for hm in hm128], axis=0)
            e["a"] = _dot_nt(qt.astype(BF16), kst.astype(BF16))
        for i, e in enumerate(E):
            vj = e["v"][r0 - SUB:r0, :]
            vst = jnp.concatenate([jnp.where(hm, vj, 0.0) for hm in hm256], axis=0)
            o_s[i, r0:, :] += _dot(e["a"].astype(BF16), vst.astype(BF16))
    for e in E:
        e["o_diag"] = jnp.zeros((L, GLA_VW), F32)
    for d in range(SUB):
        ok = (rowi % SUB) + d < SUB
        for e in E:
            q, k, bc = e["q"], e["k"], e["bc"]
            qd = q if d == 0 else pltpu.roll(q, L - d, 0)
            bcd = bc if d == 0 else pltpu.roll(bc, L - d, 0)
            p = jnp.where(ok, qd * k * jnp.exp(jnp.where(ok, bcd - bc, 0.0)), 0.0)
            e["u"] = _dot(p.astype(BF16), segexp)
        for e in E:
            u = e["u"] * e["v"]
            e["o_diag"] = e["o_diag"] + (u if d == 0 else pltpu.roll(u, d, 0))
    for e in E:
        bc = e["bc"]
        last = bc[L - 1:L, :]
        e["upd"] = _dot_tn((e["k"] * jnp.exp(last - bc)).astype(BF16), e["vb"])
    for i, e in enumerate(E):
        e["o"] = o_s[i] + e["o_diag"]
        e["ms"] = _dot_exact_r(e["o"] * e["o"], segmean, 2)
    for i, e in enumerate(E):
        last_col = e["bc"].T[:, L - 1:L]
        s_s[i] = e["S"] * jnp.exp(last_col) + jnp.where(bd, e["upd"], 0.0)
        r = e["r"]
        y = e["o"] * lax.rsqrt(e["ms"] + NORM_EPS) * ng * (r * _sigmoid(r))
        if single:
            y_ref[i:i + 1, :] = y[0:1, :].astype(y_ref.dtype)
        else:
            y_ref[i] = y.astype(y_ref.dtype)

    @pl.when(c == nc - 1)
    def _():
        for i in range(bb):
            so_ref[i] = _dot_exact_r(s_s[i], gather_m, 3)


def gla(z, zg, wg, bg, state, ng, bb):
    single = z.ndim == 2
    B = z.shape[0]
    nc = 1 if single else z.shape[1] // CHUNK
    zw = 2 * GLA_KW + 2 * GLA_VW
    if single:
        tok = lambda n: pl.BlockSpec((bb, n), lambda b, c: (b, 0))
        y_shape = jax.ShapeDtypeStruct((B, GLA_VW), F32)
    else:
        tok = lambda n: pl.BlockSpec((bb, CHUNK, n), lambda b, c: (b, c, 0))
        y_shape = jax.ShapeDtypeStruct((B, z.shape[1], GLA_VW), BF16)
    s_spec = pl.BlockSpec((bb, GLA_KW, GLA_DV), lambda b, c: (b, 0, 0))
    has_state = state is not None
    return pl.pallas_call(
        functools.partial(_gla_kernel, 1 if single else CHUNK, bb, single, has_state),
        out_shape=[y_shape, jax.ShapeDtypeStruct((B, GLA_KW, GLA_DV), F32)],
        grid=(B // bb, nc),
        in_specs=[tok(zw), tok(128),
                  pl.BlockSpec((128, GLA_KW), lambda b, c: (0, 0)),
                  pl.BlockSpec((1, GLA_KW), lambda b, c: (0, 0))]
                 + ([s_spec] if has_state else []) + [pl.BlockSpec((1, GLA_VW), lambda b, c: (0, 0))],
        out_specs=[tok(GLA_VW), s_spec],
        scratch_shapes=[pltpu.VMEM((bb, GLA_KW, GLA_VW), F32), pltpu.VMEM((bb, CHUNK, GLA_VW), F32)],
        compiler_params=_cp("parallel", "arbitrary"),
        name="gla",
    )(z, zg, wg, bg, *((state,) if has_state else ()), ng)


def _fox_decode_kernel(nps, pt_ref, q_ref, kn_ref, vn_ref, zgn_ref, *refs):
    k_refs = refs[0:nps]
    v_refs = refs[nps:2 * nps]
    lf_refs = refs[2 * nps:3 * nps]
    o_ref, lfo_ref = refs[3 * nps:3 * nps + 2]
    qbd_s, m_s, l_s, acc_s, cc_s = refs[3 * nps + 2:]
    st = pl.program_id(1)
    ns = pl.num_programs(1)
    P = PAGE_SIZE
    eye8 = _iota((8, 128), 0) == _iota((8, 128), 1)
    head8 = _iota((8, FOX_W), 0) == _iota((8, FOX_W), 1) // FOX_DH

    @pl.when(st == 0)
    def _():
        qbd = jnp.where(_iota((16, FOX_W), 0) == _iota((16, FOX_W), 1) // FOX_DH,
                        jnp.broadcast_to(q_ref[...].astype(F32), (16, FOX_W)), 0.0)
        qbd_s[...] = qbd.astype(BF16)
        m_s[...] = jnp.full_like(m_s, NEG_INF)
        l_s[...] = jnp.zeros_like(l_s)
        acc_s[...] = jnp.zeros_like(acc_s)
        cc_s[...] = jnp.zeros_like(cc_s)

    triu = (_iota((P, P), 0) <= _iota((P, P), 1)).astype(BF16)
    qbd = qbd_s[...]
    local = _dot_exact_r(jnp.concatenate([r[...] for r in lf_refs], axis=0), triu, 3)
    carry = cc_s[...]
    s_parts = []
    for i in range(nps):
        loc = local[8 * i:8 * (i + 1), :]
        s_parts.append(_dot(qbd, k_refs[i][...].astype(BF16))[0:8, :] - (loc + carry))
        carry = carry + loc[:, P - 1:P]
    cc_s[...] = carry
    m_loc = s_parts[0]
    for sp in s_parts[1:]:
        m_loc = jnp.maximum(m_loc, sp)
    m = m_s[...]
    m_new = jnp.maximum(m, jnp.max(m_loc, axis=1, keepdims=True))
    a = jnp.exp(m - m_new)
    m_s[...] = m_new
    p_parts = [jnp.exp(sp - m_new) for sp in s_parts]
    p_sum = p_parts[0]
    for pp in p_parts[1:]:
        p_sum = p_sum + pp
    l_s[...] = a * l_s[...] + jnp.sum(p_sum, axis=1, keepdims=True)
    o_even = jnp.zeros((16, FOX_W), F32)
    o_odd = jnp.zeros((16, FOX_W), F32)
    for i in range(0, nps, 2):
        pp = jnp.concatenate([p_parts[i], p_parts[i + 1]], axis=0).astype(BF16)
        o_even = o_even + _dot_nt(pp, v_refs[i][...].astype(BF16))
        o_odd = o_odd + _dot_nt(pp, v_refs[i + 1][...].astype(BF16))
    acc_s[...] = acc_s[...] * a + o_even[0:8, :] + o_odd[8:16, :]

    @pl.when(st == ns - 1)
    def _():
        lfn = _log_sigmoid(zgn_ref[...])
        lfo_ref[...] = lfn
        lfn_col = jnp.sum(jnp.where(eye8, jnp.broadcast_to(lfn, (8, 128)), 0.0), axis=1, keepdims=True)
        q8 = qbd_s[...].astype(F32)[0:8, :]
        s_n = jnp.sum(q8 * kn_ref[...], axis=1, keepdims=True) - (cc_s[...] + lfn_col)
        m = m_s[...]
        m_f = jnp.maximum(m, s_n)
        a = jnp.exp(m - m_f)
        p_n = jnp.exp(s_n - m_f)
        l_f = a * l_s[...] + p_n
        acc = (acc_s[...] * a + p_n * vn_ref[...]) / l_f
        o_ref[...] = jnp.sum(jnp.where(head8, acc, 0.0), axis=0, keepdims=True).astype(o_ref.dtype)


def fox_decode(page_table, q3, kn3, vn3, zgn3, pool_kT, pool_vT, pool_lfT, layer, n_pool, nps):
    DB, n_pages = page_table.shape
    base = layer * n_pool
    row = lambda n: pl.BlockSpec((None, 1, n), lambda b, s, pt: (b, 0, 0))

    def page_spec(i, r):
        return pl.BlockSpec((None, r, PAGE_SIZE), lambda b, s, pt: (base + pt[b, s * nps + i], 0, 0))

    in_specs = ([row(FOX_W), row(FOX_W), row(FOX_W), row(128)]
                + [page_spec(i, FOX_W) for i in range(nps)]
                + [page_spec(i, FOX_W) for i in range(nps)]
                + [page_spec(i, FOX_HEADS) for i in range(nps)])
    grid_spec = pltpu.PrefetchScalarGridSpec(
        num_scalar_prefetch=1,
        grid=(DB, n_pages // nps),
        in_specs=in_specs,
        out_specs=[row(FOX_W), row(128)],
        scratch_shapes=[pltpu.VMEM((16, FOX_W), BF16), pltpu.VMEM((8, 1), F32), pltpu.VMEM((8, 1), F32),
                        pltpu.VMEM((8, FOX_W), F32), pltpu.VMEM((8, 1), F32)])
    return pl.pallas_call(
        functools.partial(_fox_decode_kernel, nps),
        out_shape=[jax.ShapeDtypeStruct((DB, 1, FOX_W), BF16), jax.ShapeDtypeStruct((DB, 1, 128), F32)],
        grid_spec=grid_spec,
        compiler_params=_cp("parallel", "arbitrary"),
        name="fox_decode",
    )(page_table, q3, kn3, vn3, zgn3, *([pool_kT] * nps), *([pool_vT] * nps), *([pool_lfT] * nps))


def _route_rows(lg):
    lane_i = _iota((1, 128), 1)
    is_g = (lane_i >= _R_GRP) & (lane_i < _R_GRP + N_GROUPS)
    is_e = lane_i < N_EXPERTS
    lane = lane_i.astype(F32)
    lane_grp = (lane_i // EXPERTS_PER_GROUP).astype(F32)
    big = 4.0 * 128
    gl = jnp.where(is_g, lg, NEG_INF)
    gmax = jnp.max(gl, axis=1, keepdims=True)
    g_sel = jnp.min(jnp.where(gl == gmax, lane - _R_GRP, big), axis=1, keepdims=True)
    p_grp = 1.0 / jnp.sum(jnp.exp(gl - gmax), axis=1, keepdims=True)
    ev = jnp.where(is_e & (lane_grp == g_sel), lg, NEG_INF)
    v1 = jnp.max(ev, axis=1, keepdims=True)
    i1 = jnp.min(jnp.where(ev == v1, lane, big), axis=1, keepdims=True)
    ev2 = jnp.where(lane == i1, NEG_INF, ev)
    v2 = jnp.max(ev2, axis=1, keepdims=True)
    i2 = jnp.min(jnp.where(ev2 == v2, lane, big), axis=1, keepdims=True)
    t = jnp.exp(v2 - v1)
    w1 = p_grp / (1.0 + t)
    w2 = p_grp * t / (1.0 + t)
    return jnp.where(lane == i1, w1, 0.0) + jnp.where(lane == i2, w2, 0.0)


def _outproj_kernel(yml_ref, ygla_ref, yfox_ref, h_ref, w_ref, g_ref, b_ref, wr_ref, br_ref, h1_ref, comb_ref):
    mix = (_dot(yml_ref[...].astype(BF16), w_ref[0:ML_W, :])
           + _dot(ygla_ref[...].astype(BF16), w_ref[ML_W:ML_W + GLA_VW, :])
           + _dot(yfox_ref[...].astype(BF16), w_ref[ML_W + GLA_VW:, :]))
    h1 = _ln_rows(DN_ALPHA * h_ref[...] + mix, g_ref[...], b_ref[...])
    h1_ref[...] = h1
    comb_ref[...] = _route_rows(_dot(h1.astype(BF16), wr_ref[...]) + br_ref[...])


def outproj(yml, ygla, yfox, h, w, g, b, wr, br, tm):
    T = h.shape[0]
    tok = lambda n: pl.BlockSpec((tm, n), lambda i: (i, 0))
    full = lambda r, n: pl.BlockSpec((r, n), lambda i: (0, 0))
    return pl.pallas_call(
        _outproj_kernel,
        out_shape=[jax.ShapeDtypeStruct((T, D_MODEL), F32), jax.ShapeDtypeStruct((T, 128), F32)],
        grid=(T // tm,),
        in_specs=[tok(ML_W), tok(GLA_VW), tok(FOX_W), tok(D_MODEL), full(D_MODEL, D_MODEL),
                  full(1, D_MODEL), full(1, D_MODEL), full(D_MODEL, 128), full(1, 128)],
        out_specs=[tok(D_MODEL), tok(128)],
        compiler_params=_cp("parallel"),
        name="outproj",
    )(yml, ygla, yfox, h, w, g, b, wr, br)


def _moe_kernel(x_ref, comb_ref, wg_ref, wu_ref, wd_ref, g_ref, b_ref, o_ref, acc_s, xb_s):
    grp = pl.program_id(1)

    @pl.when(grp == 0)
    def _():
        acc_s[...] = jnp.zeros_like(acc_s)
        xb_s[...] = x_ref[...].astype(BF16)

    xb = xb_s[...]
    lane = _iota((1, 128), 1)
    comb = comb_ref[...]
    E = EXPERTS_PER_GROUP

    def up(j):
        return _dot(xb, wg_ref[j].astype(BF16)), _dot(xb, wu_ref[j].astype(BF16))

    au = [up(0)]
    out = None
    for j in range(E):
        if j + 1 < E:
            au.append(up(j + 1))
        a, u = au[j]
        ce = jnp.sum(jnp.where(lane == grp * E + j, comb, 0.0), axis=1, keepdims=True)
        hdn = (a * _sigmoid(a)) * u * ce
        d = _dot(hdn.astype(BF16), wd_ref[j].astype(BF16))
        out = d if out is None else out + d
    acc_s[...] += out

    @pl.when(grp == pl.num_programs(1) - 1)
    def _():
        o_ref[...] = _ln_rows(DN_ALPHA * x_ref[...] + acc_s[...], g_ref[...], b_ref[...])


def moe(x, comb, wg, wu, wd, layer, g, b, tm):
    T = x.shape[0]
    E = EXPERTS_PER_GROUP
    return pl.pallas_call(
        _moe_kernel,
        out_shape=jax.ShapeDtypeStruct((T, D_MODEL), F32),
        grid=(T // tm, N_GROUPS),
        in_specs=[pl.BlockSpec((tm, D_MODEL), lambda i, e: (i, 0)),
                  pl.BlockSpec((tm, 128), lambda i, e: (i, 0)),
                  pl.BlockSpec((None, E, D_MODEL, D_EXPERT), lambda i, e: (layer, e, 0, 0)),
                  pl.BlockSpec((None, E, D_MODEL, D_EXPERT), lambda i, e: (layer, e, 0, 0)),
                  pl.BlockSpec((None, E, D_EXPERT, D_MODEL), lambda i, e: (layer, e, 0, 0)),
                  pl.BlockSpec((1, D_MODEL), lambda i, e: (0, 0)),
                  pl.BlockSpec((1, D_MODEL), lambda i, e: (0, 0))],
        out_specs=pl.BlockSpec((tm, D_MODEL), lambda i, e: (i, 0)),
        scratch_shapes=[pltpu.VMEM((tm, D_MODEL), F32), pltpu.VMEM((tm, D_MODEL), BF16)],
        compiler_params=_cp("parallel", "arbitrary"),
        name="moe",
    )(x, comb, wg, wu, wd, g, b)


def _perm_index():
    idx = []
    for name, n in (("ml_q", 256), ("ml_k", 256), ("ml_v", 256), ("ml_o", 256),
                    ("g_q", 128), ("g_k", 128), ("g_v", 256), ("g_r", 256),
                    ("fx_q", 512), ("fx_k", 512), ("fx_v", 512),
                    ("fx_f", 8), ("ml_i", 4), ("ml_f", 4), ("g_lr", 16)):
        idx.extend(range(_OFF[name], _OFF[name] + n))
    return np.asarray(idx, np.int32)


def _col_scale():
    s = np.ones((_NPAD,), np.float32)
    s[_ML0 + ML_W:_ML0 + 2 * ML_W] = ML_DH ** -0.5
    s[_FQ0:_FK0] = FOX_DH ** -0.5
    return s


def _prep_inproj(w_in, b_in):
    idx = _perm_index()
    pad = _NPAD - idx.shape[0]
    scale = jnp.asarray(_col_scale())
    w = jnp.pad(jnp.take(w_in, idx, axis=2), ((0, 0), (0, 0), (0, pad))) * scale
    b = jnp.pad(jnp.take(b_in, idx, axis=1), ((0, 0), (0, pad))) * scale
    return w.astype(BF16), b.reshape(DEPTH, 1, _NPAD)


def _prep_inproj_t(w_in, b_in):
    wt = jnp.transpose(w_in, (0, 2, 1))
    parts, bparts = [], []
    for name, sc in (("fx_q", FOX_DH ** -0.5 * LOG2E), ("fx_k", 1.0), ("fx_v", 1.0)):
        parts.append(wt[:, _OFF[name]:_OFF[name] + FOX_W, :] * sc)
        bparts.append(b_in[:, _OFF[name]:_OFF[name] + FOX_W] * sc)
    return jnp.concatenate(parts, axis=1).astype(BF16), jnp.concatenate(bparts, axis=1)[:, :, None]


def kernel(x_prompt, x_sample, cache_fox_k, cache_fox_v, cache_fox_logf, state_mlstm_C, state_mlstm_n, state_mlstm_m, state_gla_S, page_table, ln_in_g, ln_in_b, w_in, b_in, mlstm_norm_g, gla_w_gate_up, gla_b_gate, gla_norm_g, w_out, ln1_g, ln1_b, w_router_group, b_router_group, w_router_expert, b_router_expert, w_exp_gate, w_exp_up, w_exp_down, ln2_g, ln2_b):
    B, S, D = x_prompt.shape
    DB = x_sample.shape[0]
    T = B * S
    n_pool = cache_fox_k.shape[1]

    w_in_p, b_in_p = _prep_inproj(w_in, b_in)
    w_out_b = w_out.astype(BF16)
    wr = jnp.zeros((DEPTH, D, 128), F32)
    wr = wr.at[:, :, _R_GRP:_R_GRP + N_GROUPS].set(w_router_group).at[:, :, _R_EXP:_R_EXP + N_EXPERTS].set(w_router_expert)
    wr = wr.astype(BF16)
    br = jnp.zeros((DEPTH, 1, 128), F32)
    br = br.at[:, 0, _R_GRP:_R_GRP + N_GROUPS].set(b_router_group).at[:, 0, _R_EXP:_R_EXP + N_EXPERTS].set(b_router_expert)
    wg_b, wu_b, wd_b = w_exp_gate, w_exp_up, w_exp_down
    wgl = jnp.zeros((DEPTH, 128, GLA_KW), F32).at[:, _G_LR:_G_LR + GLA_RANK, :].set(gla_w_gate_up).astype(BF16)
    row = lambda a: a.reshape(DEPTH, 1, -1)
    ln1g, ln1b, ln2g, ln2b = row(ln1_g), row(ln1_b), row(ln2_g), row(ln2_b)
    mlng, glng, glbg = row(mlstm_norm_g), row(gla_norm_g), row(gla_b_gate)
    w_in_t, b_in_t = _prep_inproj_t(w_in, b_in)
    pool_k = jnp.transpose(cache_fox_k, (0, 1, 3, 4, 2)).reshape(DEPTH * n_pool, FOX_W, PAGE_SIZE)
    pool_v = jnp.transpose(cache_fox_v, (0, 1, 3, 4, 2)).reshape(DEPTH * n_pool, FOX_W, PAGE_SIZE)
    pool_lf = jnp.transpose(cache_fox_logf, (0, 1, 3, 2)).reshape(DEPTH * n_pool, FOX_HEADS, PAGE_SIZE)

    g_in, b_in_ln = ln_in_g.reshape(1, D), ln_in_b.reshape(1, D)
    hp = ln_rows(x_prompt.reshape(T, D), g_in, b_in_ln, 512)
    hs = ln_rows(x_sample.reshape(DB, D), g_in, b_in_ln, DB)

    outs = {k: [] for k in ("lfp", "ks", "vs", "lfs", "cp", "np", "mp", "cs", "ns", "ms", "gp", "gs")}
    kv_all = None
    tq = min(FOX_TQ, S)
    for l in range(DEPTH):
        zml, zgla, zg, k_rm, qT, kT_all, vT_all = inproj_t(hp, w_in_p, b_in_p[l], w_in_t, b_in_t[l],
                                                           kv_all, l, B, S, min(INPROJ_TM, S))
        kv_all = (kT_all, vT_all)
        zg3 = zg.reshape(B, S, 128)
        lf, c_sh = foxgate(zg3)
        y_fox = fox_attn(qT, k_rm, vT_all, c_sh, l, B, S, tq)
        y_ml, c_o, n_o, m_o = mlstm(zml.reshape(B, S, 4 * ML_W), zg3, None, mlng[l], min(MLSTM_BB, B))
        y_gla, s_o = gla(zgla.reshape(B, S, -1), zg3, wgl[l], glbg[l], None, glng[l], min(GLA_BB, B))
        h1, comb = outproj(y_ml.reshape(T, ML_W), y_gla.reshape(T, GLA_VW), y_fox, hp, w_out_b[l],
                           ln1g[l], ln1b[l], wr[l], br[l], 512)
        hp = moe(h1, comb, wg_b, wu_b, wd_b, l, ln2g[l], ln2b[l], min(1024, T))
        outs["lfp"].append(lf)
        outs["cp"].append(c_o); outs["np"].append(n_o); outs["mp"].append(m_o[:, 0, :ML_HEADS]); outs["gp"].append(s_o)

        zml, zgla, fq, fk, fv, zg = inproj(hs, w_in_p, l, b_in_p[l], DB)
        ml_state = (state_mlstm_C[l].astype(F32).reshape(DB, ML_W, ML_DH),
                    state_mlstm_n[l].astype(F32).reshape(DB, 1, ML_W),
                    jnp.pad(state_mlstm_m[l].astype(F32), ((0, 0), (0, 128 - ML_HEADS))).reshape(DB, 1, 128))
        sbb = min(SAMPLE_BB, DB)
        y_ml, c_o, n_o, m_o = mlstm(zml, zg, ml_state, mlng[l], sbb)
        y_gla, s_o = gla(zgla, zg, wgl[l], glbg[l], state_gla_S[l].astype(F32).reshape(DB, GLA_KW, GLA_DV),
                         glng[l], sbb)
        y_fox, lfn = fox_decode(page_table, fq.reshape(DB, 1, FOX_W), fk.reshape(DB, 1, FOX_W),
                                fv.reshape(DB, 1, FOX_W), zg.reshape(DB, 1, 128), pool_k, pool_v, pool_lf,
                                l, n_pool, min(DECODE_PAGES, page_table.shape[1]))
        h1, comb = outproj(y_ml, y_gla, y_fox.reshape(DB, FOX_W), hs, w_out_b[l],
                           ln1g[l], ln1b[l], wr[l], br[l], DB)
        hs = moe(h1, comb, wg_b, wu_b, wd_b, l, ln2g[l], ln2b[l], DB)
        outs["ks"].append(fk); outs["vs"].append(fv); outs["lfs"].append(lfn[:, 0, :FOX_HEADS])
        outs["cs"].append(c_o); outs["ns"].append(n_o); outs["ms"].append(m_o[:, 0, :ML_HEADS]); outs["gs"].append(s_o)

    st = lambda k, shape, dt: jnp.stack(outs[k]).reshape((DEPTH,) + shape).astype(dt)
    kd, vd, ld = cache_fox_k.dtype, cache_fox_v.dtype, cache_fox_logf.dtype
    cd, nd, md, sd = state_mlstm_C.dtype, state_mlstm_n.dtype, state_mlstm_m.dtype, state_gla_S.dtype
    kvp = lambda a, dt: jnp.transpose(a.reshape(DEPTH, B, FOX_HEADS, FOX_DH, S), (0, 1, 4, 2, 3)).astype(dt)
    lfp = jnp.transpose(jnp.stack(outs["lfp"]), (0, 1, 3, 2)).astype(ld)
    return (hp.reshape(B, S, D), hs.reshape(DB, 1, D),
            kvp(kv_all[0], kd), kvp(kv_all[1], vd), lfp,
            st("ks", (DB, 1, FOX_HEADS, FOX_DH), kd), st("vs", (DB, 1, FOX_HEADS, FOX_DH), vd),
            st("lfs", (DB, 1, FOX_HEADS), ld),
            st("cp", (B, ML_HEADS, ML_DH, ML_DH), cd), st("np", (B, ML_HEADS, ML_DH), nd), st("mp", (B, ML_HEADS), md),
            st("cs", (DB, ML_HEADS, ML_DH, ML_DH), cd), st("ns", (DB, ML_HEADS, ML_DH), nd), st("ms", (DB, ML_HEADS), md),
            st("gp", (B, GLA_HEADS, GLA_DK, GLA_DV), sd), st("gs", (DB, GLA_HEADS, GLA_DK, GLA_DV), sd))
```
